```python
import math
import jax, jax.numpy as jnp
from jax import lax
import numpy as np

D_MODEL = 1024
BATCH = 8
SEQ = 2048
DEPTH = 2

N_HEADS = 16
HEAD_DIM = D_MODEL // N_HEADS
D_FF = ((8 * D_MODEL // 3 + 255) // 256) * 256
CONV_WIDTH = 3
N_MIXERS = 2
MOBA_BLOCK = 256
MOBA_TOPK = 3
MOBA_QCHUNK = 32
SB_QBLOCK = 128
REL_BUCKETS = 32
REL_MAX_DIST = 128
NORM_EPS = 1e-6
NEG = -1e30

kernel_name = "moba_stickbreaking_convffn_hybrid"


def rms_norm(x, g):
    xf = x.astype(jnp.float32)
    y = xf * lax.rsqrt(jnp.mean(xf * xf, axis=-1, keepdims=True) + NORM_EPS)
    return (y * g.astype(jnp.float32)).astype(x.dtype)


def rel_bucket(dist):
    n = jnp.maximum(dist, 0)
    max_exact = REL_BUCKETS // 2
    nf = jnp.maximum(n, 1).astype(jnp.float32)
    large = max_exact + (jnp.log(nf / max_exact) / math.log(REL_MAX_DIST / max_exact)
                         * (REL_BUCKETS - max_exact)).astype(jnp.int32)
    large = jnp.minimum(large, REL_BUCKETS - 1)
    return jnp.where(n < max_exact, n, large)


def moba_attention(q, k, v, rel_bias):
    B, H, S, Dh = q.shape
    BS = MOBA_BLOCK
    QC = MOBA_QCHUNK
    nb = -(-S // BS)
    pad = nb * BS - S
    kp = jnp.pad(k, ((0, 0), (0, 0), (0, pad), (0, 0))).reshape(B, H, nb, BS, Dh)
    vp = jnp.pad(v, ((0, 0), (0, 0), (0, pad), (0, 0))).reshape(B, H, nb, BS, Dh)
    scale = Dh ** -0.5

    pos = jnp.arange(S)
    qblk = pos // BS
    kmean = jnp.mean(kp.astype(jnp.float32), axis=3)
    gate = jnp.einsum('bhsd,bhnd->bhsn', q.astype(jnp.float32), kmean)
    past = jnp.arange(nb)[None, :] < qblk[:, None]
    gate = jnp.where(past, gate, NEG)
    kk = min(MOBA_TOPK, nb)
    _, top_idx = lax.top_k(gate, kk)
    valid = jnp.arange(kk)[None, :] < qblk[:, None]

    nc = S // QC
    q_c = q.reshape(B, H, nc, QC, Dh).transpose(2, 0, 1, 3, 4)
    idx_c = top_idx.reshape(B, H, nc, QC, kk).transpose(2, 0, 1, 3, 4)
    valid_c = valid.reshape(nc, QC, kk)
    bi = jnp.arange(B)[:, None, None]
    hi = jnp.arange(H)[None, :, None]
    hi4 = jnp.arange(H)[None, :, None, None]
    offs = jnp.arange(BS)

    def chunk(args):
        ci, qi, idx, vld = args
        qpos = ci * QC + jnp.arange(QC)
        own = (ci * QC) // BS
        k_own = lax.dynamic_index_in_dim(kp, own, axis=2, keepdims=False)
        v_own = lax.dynamic_index_in_dim(vp, own, axis=2, keepdims=False)
        kpos_own = own * BS + offs
        s_own = jnp.einsum('bhqd,bhkd->bhqk', qi, k_own).astype(jnp.float32) * scale
        s_own = s_own + rel_bias[:, rel_bucket(qpos[:, None] - kpos_own[None, :])].astype(jnp.float32)
        s_own = jnp.where(kpos_own[None, :] <= qpos[:, None], s_own, NEG)
        logits = []
        for j in range(kk):
            blk = idx[..., j]
            kg = kp[bi, hi, blk]
            s = jnp.einsum('bhqd,bhqkd->bhqk', qi, kg).astype(jnp.float32) * scale
            kpos = blk[..., None] * BS + offs
            bias = rel_bias[hi4, rel_bucket(qpos[None, None, :, None] - kpos)].astype(jnp.float32)
            logits.append(jnp.where(vld[None, None, :, j, None], s + bias, NEG))
        logits.append(s_own)
        p = jax.nn.softmax(jnp.concatenate(logits, axis=-1), axis=-1).astype(v.dtype)
        out = jnp.einsum('bhqk,bhkd->bhqd', p[..., kk * BS:], v_own)
        for j in range(kk):
            vg = vp[bi, hi, idx[..., j]]
            out = out + jnp.einsum('bhqk,bhqkd->bhqd', p[..., j * BS:(j + 1) * BS], vg)
        return out

    out = lax.map(chunk, (jnp.arange(nc), q_c, idx_c, valid_c))
    return out.transpose(1, 2, 0, 3, 4).reshape(B, H, S, Dh)


def stick_breaking_attention(q, k, v):
    B, H, S, Dh = q.shape
    QB = SB_QBLOCK
    nq = S // QB
    scale = Dh ** -0.5
    kpos = jnp.arange(S)
    q_b = q.reshape(B, H, nq, QB, Dh).transpose(2, 0, 1, 3, 4)

    def block(args):
        bidx, qi = args
        qpos = bidx * QB + jnp.arange(QB)
        z = jnp.einsum('bhqd,bhkd->bhqk', qi, k).astype(jnp.float32) * scale
        strict = kpos[None, :] < qpos[:, None]
        log_beta = jax.nn.log_sigmoid(z)
        log_1mb = jnp.where(strict, jax.nn.log_sigmoid(-z), 0.0)
        tail = lax.cumsum(log_1mb, axis=3, reverse=True) - log_1mb
        a = jnp.where(strict, jnp.exp(log_beta + tail), 0.0)
        return jnp.einsum('bhqk,bhkd->bhqd', a.astype(v.dtype), v)

    out = lax.map(block, (jnp.arange(nq), q_b))
    return out.transpose(1, 2, 0, 3, 4).reshape(B, H, S, Dh)


def conv_ffn(y, w_up, conv_w, conv_b, w_down):
    u = y @ w_up
    c = u.shape[-1]
    u = lax.conv_general_dilated(u, conv_w[:, None, :], window_strides=(1,),
                                 padding=[(CONV_WIDTH - 1, 0)],
                                 dimension_numbers=('NWC', 'WIO', 'NWC'),
                                 feature_group_count=c) + conv_b
    gate, val = jnp.split(u, 2, axis=-1)
    return (jax.nn.silu(gate) * val) @ w_down


def setup_inputs(seed: int = 0) -> dict:
    key = jax.random.key(seed)
    ks = jax.random.split(key, 12)
    D, F = D_MODEL, D_FF
    f32 = jnp.float32
    return {
        "x": jax.random.normal(ks[0], (BATCH, SEQ, D), f32),
        "attn_norm": 1.0 + 0.02 * jax.random.normal(ks[1], (DEPTH, D), f32),
        "w_qkv": jax.random.normal(ks[2], (DEPTH, D, 3 * D), f32) * D ** -0.5,
        "w_o": jax.random.normal(ks[3], (DEPTH, D, D), f32) * D ** -0.5,
        "rel_bias": 0.5 * jax.random.normal(ks[4], (N_HEADS, REL_BUCKETS), f32),
        "ffn_norm": 1.0 + 0.02 * jax.random.normal(ks[5], (DEPTH, D), f32),
        "w_up": jax.random.normal(ks[6], (DEPTH, D, 2 * F), f32) * D ** -0.5,
        "conv_w": jax.random.normal(ks[7], (DEPTH, CONV_WIDTH, 2 * F), f32) * CONV_WIDTH ** -0.5,
        "conv_b": 0.02 * jax.random.normal(ks[8], (DEPTH, 2 * F), f32),
        "w_down": jax.random.normal(ks[9], (DEPTH, F, D), f32) * F ** -0.5,
        "final_norm": 1.0 + 0.02 * jax.random.normal(ks[10], (D,), f32),
    }


def reference(x, attn_norm, w_qkv, w_o, rel_bias, ffn_norm, w_up, conv_w, conv_b, w_down, final_norm):
    B, S, D = x.shape
    h = x
    for i in range(DEPTH):
        y = rms_norm(h, attn_norm[i])
        qkv = (y @ w_qkv[i]).reshape(B, S, 3, N_HEADS, HEAD_DIM)
        q = qkv[:, :, 0].transpose(0, 2, 1, 3)
        k = qkv[:, :, 1].transpose(0, 2, 1, 3)
        v = qkv[:, :, 2].transpose(0, 2, 1, 3)
        if i % N_MIXERS == 0:
            o = moba_attention(q, k, v, rel_bias)
        else:
            o = stick_breaking_attention(q, k, v)
        h = h + o.transpose(0, 2, 1, 3).reshape(B, S, D) @ w_o[i]
        y = rms_norm(h, ffn_norm[i])
        h = h + conv_ffn(y, w_up[i], conv_w[i], conv_b[i], w_down[i])
    return rms_norm(h, final_norm)
```

```python
import functools

import jax
import jax.numpy as jnp
from jax import lax
from jax.experimental import pallas as pl
from jax.experimental.pallas import tpu as pltpu

F32 = jnp.float32
BF16 = jnp.bfloat16

N_HEADS = 16
HEAD_DIM = 64
HEADS_PER_STEP = 2
LANES = HEADS_PER_STEP * HEAD_DIM
MOBA_BLOCK = 256
MOBA_TOPK = 3
ATT_TILE = MOBA_BLOCK
CONV_WIDTH = 3
REL_BUCKETS = 32
NORM_EPS = 1e-6
NEG = -1e30
ROW_TILE = 512
FFN_CHUNK = 256
HALO = 16
VMEM_LIMIT = 56 * 1024 * 1024

REL_BUCKET_START = (0, 1, 2, 3, 4, 5, 6, 7, 8, 9, 10, 11, 12, 13, 14, 15, 16,
                    19, 21, 24, 27, 31, 35, 40, 46, 52, 59, 67, 77, 87, 99, 113)


def _dot(a, b):
    return jnp.dot(a, b, preferred_element_type=F32)


def _dot_nt(a, b):
    return lax.dot_general(a, b, (((1,), (1,)), ((), ())), preferred_element_type=F32)


def _rms_norm(x, g):
    ms = jnp.mean(x * x, axis=-1, keepdims=True)
    return x * lax.rsqrt(ms + NORM_EPS) * g


def _split_bf16(x):
    hi = x.astype(BF16)
    lo = (x - hi.astype(F32)).astype(BF16)
    return hi, lo


def _params(n_axes):
    return pltpu.CompilerParams(
        dimension_semantics=("arbitrary",) * n_axes,
        vmem_limit_bytes=VMEM_LIMIT)


def _norm_qkv_kernel(x_ref, g_ref, w_ref, q_ref, k_ref, v_ref, *, d_model, q_scale):
    y = _rms_norm(x_ref[...], g_ref[...]).astype(BF16)
    for idx, out in enumerate((q_ref, k_ref, v_ref)):
        r = _dot(y, w_ref[:, idx * d_model:(idx + 1) * d_model])
        if idx == 0:
            r = r * q_scale
        out[...] = r.astype(BF16)


def _norm_qkv(h, gain, w_bf16):
    m, d = h.shape
    row = pl.BlockSpec((ROW_TILE, d), lambda i: (i, 0))
    out = jax.ShapeDtypeStruct((m, d), BF16)
    return pl.pallas_call(
        functools.partial(_norm_qkv_kernel, d_model=d, q_scale=HEAD_DIM ** -0.5),
        grid=(m // ROW_TILE,),
        in_specs=[row,
                  pl.BlockSpec((1, d), lambda i: (0, 0)),
                  pl.BlockSpec((d, 3 * d), lambda i: (0, 0))],
        out_specs=[row, row, row],
        out_shape=[out, out, out],
        compiler_params=_params(1),
        name="norm_qkv",
    )(h, gain.reshape(1, d), w_bf16)


def _bias_table_kernel(rb_ref, out_ref):
    h = pl.program_id(0)
    key = lax.broadcasted_iota(jnp.int32, (ATT_TILE, ATT_TILE), 0)
    qry = lax.broadcasted_iota(jnp.int32, (ATT_TILE, ATT_TILE), 1)
    for delta in range(3):
        dist = jnp.maximum(qry - key + ATT_TILE * delta, 0)
        val = jnp.full((ATT_TILE, ATT_TILE), rb_ref[h, 0], F32)
        for b in range(1, REL_BUCKETS):
            val = jnp.where(dist >= REL_BUCKET_START[b], rb_ref[h, b], val)
        out_ref[0, delta] = val


def _bias_table(rel_bias):
    return pl.pallas_call(
        _bias_table_kernel,
        grid=(N_HEADS,),
        in_specs=[pl.BlockSpec(memory_space=pltpu.SMEM)],
        out_specs=pl.BlockSpec((1, 3, ATT_TILE, ATT_TILE), lambda h: (h, 0, 0, 0)),
        out_shape=jax.ShapeDtypeStruct((N_HEADS, 3, ATT_TILE, ATT_TILE), F32),
        compiler_params=_params(1),
        name="rel_bias_table",
    )(rel_bias)


def _store_v_transposed(v_ref, vt_ref, n_blocks):
    for j in range(n_blocks):
        vj = v_ref[0, j * ATT_TILE:(j + 1) * ATT_TILE, :].astype(F32)
        vt_ref[j] = vj.T.astype(BF16)


def _head_query(q2, hd):
    lane = lax.broadcasted_iota(jnp.int32, q2.shape, 1)
    keep = (lane < HEAD_DIM) if hd == 0 else (lane >= HEAD_DIM)
    return jnp.where(keep, q2, jnp.zeros_like(q2))


def _key_tile(k_ref, j):
    off = pl.multiple_of(j * ATT_TILE, ATT_TILE)
    return k_ref[0, pl.ds(off, ATT_TILE), :]


def _attention_call(kernel, q, k, v, extra_inputs, extra_specs, scratch, name):
    b, s, d = q.shape
    n_pairs = d // LANES
    n_q = s // ATT_TILE
    qo_spec = pl.BlockSpec((1, ATT_TILE, LANES), lambda hp, bb, n: (bb, n, hp))
    kv_spec = pl.BlockSpec((1, s, LANES), lambda hp, bb, n: (bb, 0, hp))
    return pl.pallas_call(
        kernel,
        grid=(n_pairs, b, n_q),
        in_specs=[qo_spec, kv_spec, kv_spec] + extra_specs,
        out_specs=qo_spec,
        out_shape=jax.ShapeDtypeStruct((b, s, d), BF16),
        scratch_shapes=scratch,
        compiler_params=_params(3),
        name=name,
    )(q, k, v, *extra_inputs)


def _moba_kernel(q_ref, k_ref, v_ref, bias_ref, o_ref,
                 vt_ref, kmean_ref, sel_ref, logit_ref, *, n_blocks):
    n = pl.program_id(2)

    @pl.when(n == 0)
    def _():
        _store_v_transposed(v_ref, vt_ref, n_blocks)
        rows = []
        for j in range(n_blocks):
            kj = k_ref[0, j * ATT_TILE:(j + 1) * ATT_TILE, :].astype(F32)
            rows.append(jnp.sum(kj, axis=0, keepdims=True) * (1.0 / ATT_TILE))
        rows.append(jnp.zeros((2 * 8 - n_blocks, LANES), F32))
        kmean_ref[...] = jnp.concatenate(rows, axis=0)

    q2 = q_ref[0]
    km_hi, km_lo = _split_bf16(kmean_ref[...])
    blk = lax.broadcasted_iota(jnp.int32, (n_blocks, ATT_TILE), 0)
    key = lax.broadcasted_iota(jnp.int32, (ATT_TILE, ATT_TILE), 0)
    qry = lax.broadcasted_iota(jnp.int32, (ATT_TILE, ATT_TILE), 1)
    causal = key <= qry
    outs = []
    for hd in range(HEADS_PER_STEP):
        qh = _head_query(q2, hd)

        gate = (_dot_nt(km_hi, qh) + _dot_nt(km_lo, qh))[0:n_blocks]
        gate = jnp.where(blk < n, gate, NEG)
        for j in range(n_blocks):
            gj = gate[j:j + 1, :]
            beats = (gate > gj) | ((gate == gj) & (blk < j))
            rank = jnp.sum(beats.astype(jnp.int32), axis=0, keepdims=True)
            chosen = (rank < MOBA_TOPK) & (j < n)
            sel_ref[j] = jnp.broadcast_to(chosen.astype(F32), (8, ATT_TILE))

        def past_block(j, m, qh=qh, hd=hd):
            s = _dot_nt(_key_tile(k_ref, j), qh)
            bias = bias_ref[hd, jnp.minimum(n - j, 2)]
            chosen = sel_ref[j][0:1, :] > 0.5
            logit = jnp.where(chosen, s + bias, NEG)
            logit_ref[pl.ds(pl.multiple_of(j * ATT_TILE, ATT_TILE), ATT_TILE), :] = logit
            return jnp.maximum(m, jnp.max(logit, axis=0, keepdims=True))

        m = lax.fori_loop(0, n, past_block, jnp.full((1, ATT_TILE), NEG, F32))
        s = _dot_nt(_key_tile(k_ref, n), qh)
        logit = jnp.where(causal, s + bias_ref[hd, 0], NEG)
        logit_ref[pl.ds(pl.multiple_of(n * ATT_TILE, ATT_TILE), ATT_TILE), :] = logit
        m = jnp.maximum(m, jnp.max(logit, axis=0, keepdims=True))

        def weighted_values(j, carry, m=m, hd=hd):
            denom, acc = carry
            off = pl.multiple_of(j * ATT_TILE, ATT_TILE)
            p = jnp.exp(logit_ref[pl.ds(off, ATT_TILE), :] - m)
            denom = denom + jnp.sum(p, axis=0, keepdims=True)
            vt = vt_ref[j, hd * HEAD_DIM:(hd + 1) * HEAD_DIM, :]
            return denom, acc + _dot(vt, p.astype(BF16))

        denom, acc = lax.fori_loop(
            0, n + 1, weighted_values,
            (jnp.zeros((1, ATT_TILE), F32), jnp.zeros((HEAD_DIM, ATT_TILE), F32)))
        outs.append(acc / denom)
    o_ref[0] = jnp.concatenate(outs, axis=0).T.astype(BF16)


def _moba_attention(q, k, v, bias_table):
    n_blocks = q.shape[1] // ATT_TILE
    bias_spec = pl.BlockSpec((HEADS_PER_STEP, 3, ATT_TILE, ATT_TILE),
                             lambda hp, bb, n: (hp, 0, 0, 0))
    scratch = [pltpu.VMEM((n_blocks, LANES, ATT_TILE), BF16),
               pltpu.VMEM((16, LANES), F32),
               pltpu.VMEM((n_blocks, 8, ATT_TILE), F32),
               pltpu.VMEM((n_blocks * ATT_TILE, ATT_TILE), F32)]
    return _attention_call(
        functools.partial(_moba_kernel, n_blocks=n_blocks),
        q, k, v, [bias_table], [bias_spec], scratch, "moba_attention")


def _neg_softplus(z):
    return -(jnp.maximum(z, 0.0) + jnp.log1p(jnp.exp(-jnp.abs(z))))


def _suffix_sum(tri, x):
    hi, lo = _split_bf16(x)
    return _dot(tri, hi) + _dot(tri, lo)


def _sb_kernel(q_ref, k_ref, v_ref, o_ref, vt_ref, *, n_blocks):
    n = pl.program_id(2)

    @pl.when(n == 0)
    def _():
        _store_v_transposed(v_ref, vt_ref, n_blocks)

    q2 = q_ref[0]
    key = lax.broadcasted_iota(jnp.int32, (ATT_TILE, ATT_TILE), 0)
    qry = lax.broadcasted_iota(jnp.int32, (ATT_TILE, ATT_TILE), 1)
    strict = key < qry
    tri = (qry >= key).astype(BF16)
    outs = []
    for hd in range(HEADS_PER_STEP):
        qh = _head_query(q2, hd)
        rows = slice(hd * HEAD_DIM, (hd + 1) * HEAD_DIM)

        z = _dot_nt(_key_tile(k_ref, n), qh)
        csum = _suffix_sum(tri, jnp.where(strict, _neg_softplus(z), 0.0))
        a = jnp.where(strict, jnp.exp(z + csum), 0.0)
        acc = _dot(vt_ref[n, rows, :], a.astype(BF16))
        tail = csum[0:1, :]

        def earlier_block(i, carry, qh=qh, rows=rows):
            tail, acc = carry
            j = n - 1 - i
            z = _dot_nt(_key_tile(k_ref, j), qh)
            csum = _suffix_sum(tri, _neg_softplus(z))
            a = jnp.exp(z + csum + tail)
            acc = acc + _dot(vt_ref[j, rows, :], a.astype(BF16))
            return tail + csum[0:1, :], acc

        _, acc = lax.fori_loop(0, n, earlier_block, (tail, acc))
        outs.append(acc)
    o_ref[0] = jnp.concatenate(outs, axis=0).T.astype(BF16)


def _sb_attention(q, k, v):
    n_blocks = q.shape[1] // ATT_TILE
    scratch = [pltpu.VMEM((n_blocks, LANES, ATT_TILE), BF16)]
    return _attention_call(
        functools.partial(_sb_kernel, n_blocks=n_blocks),
        q, k, v, [], [], scratch, "stick_breaking_attention")


def _wo_norm_kernel(o_ref, h_ref, w_ref, g_ref, h_out_ref, y_ref):
    h = h_ref[...] + _dot(o_ref[...], w_ref[...])
    h_out_ref[...] = h
    y_ref[...] = _rms_norm(h, g_ref[...]).astype(BF16)


def _wo_norm(o, h, w_bf16, gain):
    m, d = h.shape
    row = pl.BlockSpec((ROW_TILE, d), lambda i: (i, 0))
    return pl.pallas_call(
        _wo_norm_kernel,
        grid=(m // ROW_TILE,),
        in_specs=[row, row,
                  pl.BlockSpec((d, d), lambda i: (0, 0)),
                  pl.BlockSpec((1, d), lambda i: (0, 0))],
        out_specs=[row, row],
        out_shape=[jax.ShapeDtypeStruct((m, d), F32),
                   jax.ShapeDtypeStruct((m, d), BF16)],
        compiler_params=_params(1),
        name="wo_norm",
    )(o, h, w_bf16, gain.reshape(1, d))


def _conv_ffn_kernel(y_ref, halo_ref, h_ref, wup_ref, cw_ref, wdn_ref, gn_ref, out_ref,
                     yext_ref, acc_ref, *, tiles_per_seq, n_chunks, final_norm):
    i = pl.program_id(0)
    seq_start = (i % tiles_per_seq) == 0
    halo = halo_ref[...]
    yext_ref[0:HALO, :] = jnp.where(seq_start, jnp.zeros_like(halo), halo)
    yext_ref[HALO:, :] = y_ref[...]
    acc_ref[...] = h_ref[...]
    tm = y_ref.shape[0]

    def conv(u, taps):
        out = taps[CONV_WIDTH:CONV_WIDTH + 1, :]
        for kk in range(CONV_WIDTH):
            lo = HALO - (CONV_WIDTH - 1) + kk
            out = out + taps[kk:kk + 1, :] * u[lo:lo + tm, :]
        return out

    def chunk(c, carry):
        yext = yext_ref[...]
        gate = conv(_dot(yext, wup_ref[c]), cw_ref[c])
        val = conv(_dot(yext, wup_ref[n_chunks + c]), cw_ref[n_chunks + c])
        act = gate * (1.0 / (1.0 + jnp.exp(-gate))) * val
        acc_ref[...] += _dot(act.astype(BF16), wdn_ref[c])
        return carry

    lax.fori_loop(0, n_chunks, chunk, 0)
    h = acc_ref[...]
    out_ref[...] = _rms_norm(h, gn_ref[...]) if final_norm else h


def _conv_ffn(y, h, wup_chunks, conv_taps, wdn_chunks, gain, final_norm, seq_len):
    m, d = h.shape
    n_chunks = wdn_chunks.shape[0]
    row = pl.BlockSpec((ROW_TILE, d), lambda i: (i, 0))
    halo = pl.BlockSpec((HALO, d), lambda i: (jnp.maximum(i * (ROW_TILE // HALO) - 1, 0), 0))
    whole = lambda a: pl.BlockSpec(a.shape, lambda i: (0,) * a.ndim)
    return pl.pallas_call(
        functools.partial(_conv_ffn_kernel, tiles_per_seq=seq_len // ROW_TILE,
                          n_chunks=n_chunks, final_norm=final_norm),
        grid=(m // ROW_TILE,),
        in_specs=[row, halo, row, whole(wup_chunks), whole(conv_taps), whole(wdn_chunks),
                  pl.BlockSpec((1, d), lambda i: (0, 0))],
        out_specs=row,
        out_shape=jax.ShapeDtypeStruct((m, d), F32),
        scratch_shapes=[pltpu.VMEM((ROW_TILE + HALO, d), BF16),
                        pltpu.VMEM((ROW_TILE, d), F32)],
        compiler_params=_params(1),
        name="conv_ffn",
    )(y, y, h, wup_chunks, conv_taps, wdn_chunks, gain.reshape(1, d))


def _chunk_columns(w, chunk):
    r, c = w.shape
    return w.reshape(r, c // chunk, chunk).transpose(1, 0, 2)


def kernel(x, attn_norm, w_qkv, w_o, rel_bias, ffn_norm, w_up, conv_w, conv_b, w_down, final_norm):
    b, s, d = x.shape
    depth = w_qkv.shape[0]
    f = w_down.shape[1]
    h = x.reshape(b * s, d)
    bias_table = _bias_table(rel_bias)
    for i in range(depth):
        q, k, v = _norm_qkv(h, attn_norm[i], w_qkv[i].astype(BF16))
        q, k, v = (t.reshape(b, s, d) for t in (q, k, v))
        if i % 2 == 0:
            o = _moba_attention(q, k, v, bias_table)
        else:
            o = _sb_attention(q, k, v)
        h, y = _wo_norm(o.reshape(b * s, d), h, w_o[i].astype(BF16), ffn_norm[i])
        taps = jnp.concatenate(
            [conv_w[i], conv_b[i][None, :],
             jnp.zeros((8 - CONV_WIDTH - 1, conv_w.shape[-1]), F32)], axis=0)
        last = i == depth - 1
        h = _conv_ffn(
            y, h,
            _chunk_columns(w_up[i].astype(BF16), FFN_CHUNK),
            _chunk_columns(taps, FFN_CHUNK),
            w_down[i].astype(BF16).reshape(f // FFN_CHUNK, FFN_CHUNK, d),
            final_norm if last else attn_norm[0],
            last, s)
    return h.reshape(b, s, d)
```

```python
import functools

import jax
import jax.numpy as jnp
from jax import lax
from jax.experimental import pallas as pl
from jax.experimental.pallas import tpu as pltpu

F32 = jnp.float32
BF16 = jnp.bfloat16

N_HEADS = 16
HEAD_DIM = 64
HEADS_PER_STEP = 2
LANES = HEADS_PER_STEP * HEAD_DIM
MOBA_BLOCK = 256
MOBA_TOPK = 3
ATT_TILE = MOBA_BLOCK
CONV_WIDTH = 3
REL_BUCKETS = 32
NORM_EPS = 1e-6
NEG = -1e30
ROW_TILE = 512
FFN_CHUNK = 256
HALO = 16
VMEM_LIMIT = 56 * 1024 * 1024

REL_BUCKET_START = (0, 1, 2, 3, 4, 5, 6, 7, 8, 9, 10, 11, 12, 13, 14, 15, 16,
                    19, 21, 24, 27, 31, 35, 40, 46, 52, 59, 67, 77, 87, 99, 113)


def _dot(a, b):
    return jnp.dot(a, b, preferred_element_type=F32)


def _dot_nt(a, b):
    return lax.dot_general(a, b, (((1,), (1,)), ((), ())), preferred_element_type=F32)


def _rms_norm(x, g):
    ms = jnp.mean(x * x, axis=-1, keepdims=True)
    return x * lax.rsqrt(ms + NORM_EPS) * g


def _split_bf16(x):
    hi = x.astype(BF16)
    lo = (x - hi.astype(F32)).astype(BF16)
    return hi, lo


def _params(n_axes):
    return pltpu.CompilerParams(
        dimension_semantics=("arbitrary",) * n_axes,
        vmem_limit_bytes=VMEM_LIMIT)


def _norm_qkv_kernel(x_ref, g_ref, w_ref, q_ref, k_ref, v_ref, *, d_model, q_scale):
    y = _rms_norm(x_ref[...], g_ref[...]).astype(BF16)
    for idx, out in enumerate((q_ref, k_ref, v_ref)):
        r = _dot(y, w_ref[:, idx * d_model:(idx + 1) * d_model])
        if idx == 0:
            r = r * q_scale
        out[...] = r.astype(BF16)


def _norm_qkv(h, gain, w_bf16):
    m, d = h.shape
    row = pl.BlockSpec((ROW_TILE, d), lambda i: (i, 0))
    out = jax.ShapeDtypeStruct((m, d), BF16)
    return pl.pallas_call(
        functools.partial(_norm_qkv_kernel, d_model=d, q_scale=HEAD_DIM ** -0.5),
        grid=(m // ROW_TILE,),
        in_specs=[row,
                  pl.BlockSpec((1, d), lambda i: (0, 0)),
                  pl.BlockSpec((d, 3 * d), lambda i: (0, 0))],
        out_specs=[row, row, row],
        out_shape=[out, out, out],
        compiler_params=_params(1),
        name="norm_qkv",
    )(h, gain.reshape(1, d), w_bf16)


def _bias_table_kernel(rb_ref, out_ref):
    h = pl.program_id(0)
    key = lax.broadcasted_iota(jnp.int32, (ATT_TILE, ATT_TILE), 0)
    qry = lax.broadcasted_iota(jnp.int32, (ATT_TILE, ATT_TILE), 1)
    for delta in range(3):
        dist = jnp.maximum(qry - key + ATT_TILE * delta, 0)
        val = jnp.full((ATT_TILE, ATT_TILE), rb_ref[h, 0], F32)
        for b in range(1, REL_BUCKETS):
            val = jnp.where(dist >= REL_BUCKET_START[b], rb_ref[h, b], val)
        out_ref[0, delta] = val


def _bias_table(rel_bias):
    return pl.pallas_call(
        _bias_table_kernel,
        grid=(N_HEADS,),
        in_specs=[pl.BlockSpec(memory_space=pltpu.SMEM)],
        out_specs=pl.BlockSpec((1, 3, ATT_TILE, ATT_TILE), lambda h: (h, 0, 0, 0)),
        out_shape=jax.ShapeDtypeStruct((N_HEADS, 3, ATT_TILE, ATT_TILE), F32),
        compiler_params=_params(1),
        name="rel_bias_table",
    )(rel_bias)


def _store_v_transposed(v_ref, vt_ref, n_blocks):
    for j in range(n_blocks):
        vj = v_ref[0, j * ATT_TILE:(j + 1) * ATT_TILE, :].astype(F32)
        vt_ref[j] = vj.T.astype(BF16)


def _head_query(q2, hd):
    lane = lax.broadcasted_iota(jnp.int32, q2.shape, 1)
    keep = (lane < HEAD_DIM) if hd == 0 else (lane >= HEAD_DIM)
    return jnp.where(keep, q2, jnp.zeros_like(q2))


def _key_tile(k_ref, j):
    off = pl.multiple_of(j * ATT_TILE, ATT_TILE)
    return k_ref[0, pl.ds(off, ATT_TILE), :]


def _attention_call(kernel, q, k, v, extra_inputs, extra_specs, scratch, name):
    b, s, d = q.shape
    n_pairs = d // LANES
    n_q = s // ATT_TILE
    qo_spec = pl.BlockSpec((1, ATT_TILE, LANES), lambda hp, bb, n: (bb, n, hp))
    kv_spec = pl.BlockSpec((1, s, LANES), lambda hp, bb, n: (bb, 0, hp))
    return pl.pallas_call(
        kernel,
        grid=(n_pairs, b, n_q),
        in_specs=[qo_spec, kv_spec, kv_spec] + extra_specs,
        out_specs=qo_spec,
        out_shape=jax.ShapeDtypeStruct((b, s, d), BF16),
        scratch_shapes=scratch,
        compiler_params=_params(3),
        name=name,
    )(q, k, v, *extra_inputs)


def _moba_kernel(q_ref, k_ref, v_ref, bias_ref, o_ref,
                 vt_ref, kmean_ref, sel_ref, logit_ref, *, n_blocks):
    n = pl.program_id(2)

    @pl.when(n == 0)
    def _():
        _store_v_transposed(v_ref, vt_ref, n_blocks)
        rows = []
        for j in range(n_blocks):
            kj = k_ref[0, j * ATT_TILE:(j + 1) * ATT_TILE, :].astype(F32)
            rows.append(jnp.sum(kj, axis=0, keepdims=True) * (1.0 / ATT_TILE))
        rows.append(jnp.zeros((2 * 8 - n_blocks, LANES), F32))
        kmean_ref[...] = jnp.concatenate(rows, axis=0)
        for hd in range(HEADS_PER_STEP):
            logit_ref[hd, n_blocks] = jnp.full((ATT_TILE, ATT_TILE), NEG, F32)

    q2 = q_ref[0]
    km_hi, km_lo = _split_bf16(kmean_ref[...])
    blk = lax.broadcasted_iota(jnp.int32, (n_blocks, ATT_TILE), 0)
    key = lax.broadcasted_iota(jnp.int32, (ATT_TILE, ATT_TILE), 0)
    qry = lax.broadcasted_iota(jnp.int32, (ATT_TILE, ATT_TILE), 1)
    causal = key <= qry
    heads = range(HEADS_PER_STEP)
    qh = [_head_query(q2, hd) for hd in heads]
    rows = [slice(hd * HEAD_DIM, (hd + 1) * HEAD_DIM) for hd in heads]

    for hd in heads:
        gate = (_dot_nt(km_hi, qh[hd]) + _dot_nt(km_lo, qh[hd]))[0:n_blocks]
        gate = jnp.where(blk < n, gate, NEG)
        for j in range(n_blocks):
            gj = gate[j:j + 1, :]
            beats = (gate > gj) | ((gate == gj) & (blk < j))
            rank = jnp.sum(beats.astype(jnp.int32), axis=0, keepdims=True)
            chosen = (rank < MOBA_TOPK) & (j < n)
            sel_ref[hd, j] = jnp.broadcast_to(chosen.astype(F32), (8, ATT_TILE))

    def masked_logits(hd, j):
        s = _dot_nt(_key_tile(k_ref, j), qh[hd])
        chosen = sel_ref[hd, j][0:1, :] > 0.5
        logit = jnp.where(chosen, s + bias_ref[hd, jnp.minimum(n - j, 2)], NEG)
        logit_ref[hd, j] = logit
        return jnp.max(logit, axis=0, keepdims=True)

    def past_pair(i, ms):
        j0 = 2 * i
        j1 = jnp.minimum(j0 + 1, n - 1)
        return tuple(jnp.maximum(ms[hd], jnp.maximum(masked_logits(hd, j0), masked_logits(hd, j1)))
                     for hd in heads)

    ms = lax.fori_loop(0, (n + 1) // 2, past_pair,
                       tuple(jnp.full((1, ATT_TILE), NEG, F32) for _ in heads))
    ms = list(ms)
    for hd in heads:
        s = _dot_nt(_key_tile(k_ref, n), qh[hd])
        logit = jnp.where(causal, s + bias_ref[hd, 0], NEG)
        logit_ref[hd, n] = logit
        ms[hd] = jnp.maximum(ms[hd], jnp.max(logit, axis=0, keepdims=True))

    def weighted_pair(i, carry):
        denoms, accs = carry
        j0 = 2 * i
        j1 = j0 + 1
        j1_logits = jnp.where(j1 <= n, j1, n_blocks)
        j1_vals = jnp.minimum(j1, n)
        new_denoms, new_accs = [], []
        for hd in heads:
            p0 = jnp.exp(logit_ref[hd, j0] - ms[hd])
            p1 = jnp.exp(logit_ref[hd, j1_logits] - ms[hd])
            new_denoms.append(denoms[hd] + jnp.sum(p0, axis=0, keepdims=True)
                              + jnp.sum(p1, axis=0, keepdims=True))
            new_accs.append(accs[hd]
                            + _dot(vt_ref[j0, rows[hd], :], p0.astype(BF16))
                            + _dot(vt_ref[j1_vals, rows[hd], :], p1.astype(BF16)))
        return tuple(new_denoms), tuple(new_accs)

    denoms, accs = lax.fori_loop(
        0, (n + 2) // 2, weighted_pair,
        (tuple(jnp.zeros((1, ATT_TILE), F32) for _ in heads),
         tuple(jnp.zeros((HEAD_DIM, ATT_TILE), F32) for _ in heads)))
    out = jnp.concatenate([accs[hd] / denoms[hd] for hd in heads], axis=0)
    o_ref[0] = out.T.astype(BF16)


def _moba_attention(q, k, v, bias_table):
    n_blocks = q.shape[1] // ATT_TILE
    bias_spec = pl.BlockSpec((HEADS_PER_STEP, 3, ATT_TILE, ATT_TILE),
                             lambda hp, bb, n: (hp, 0, 0, 0))
    scratch = [pltpu.VMEM((n_blocks, LANES, ATT_TILE), BF16),
               pltpu.VMEM((16, LANES), F32),
               pltpu.VMEM((HEADS_PER_STEP, n_blocks, 8, ATT_TILE), F32),
               pltpu.VMEM((HEADS_PER_STEP, n_blocks + 1, ATT_TILE, ATT_TILE), F32)]
    return _attention_call(
        functools.partial(_moba_kernel, n_blocks=n_blocks),
        q, k, v, [bias_table], [bias_spec], scratch, "moba_attention")


def _neg_softplus(z):
    return -(jnp.maximum(z, 0.0) + jnp.log(1.0 + jnp.exp(-jnp.abs(z))))


def _suffix_sum(tri, x):
    hi, lo = _split_bf16(x)
    return _dot(tri, hi) + _dot(tri, lo)


def _sb_kernel(q_ref, k_ref, v_ref, o_ref, vt_ref, *, n_blocks):
    n = pl.program_id(2)

    @pl.when(n == 0)
    def _():
        _store_v_transposed(v_ref, vt_ref, n_blocks)
        vt_ref[n_blocks] = jnp.zeros(vt_ref.shape[1:], BF16)

    q2 = q_ref[0]
    key = lax.broadcasted_iota(jnp.int32, (ATT_TILE, ATT_TILE), 0)
    qry = lax.broadcasted_iota(jnp.int32, (ATT_TILE, ATT_TILE), 1)
    strict = key < qry
    tri = (qry >= key).astype(BF16)
    heads = range(HEADS_PER_STEP)
    qh = [_head_query(q2, hd) for hd in heads]
    rows = [slice(hd * HEAD_DIM, (hd + 1) * HEAD_DIM) for hd in heads]

    tails, accs = [], []
    for hd in heads:
        z = _dot_nt(_key_tile(k_ref, n), qh[hd])
        csum = _suffix_sum(tri, jnp.where(strict, _neg_softplus(z), 0.0))
        a = jnp.where(strict, jnp.exp(z + csum), 0.0)
        accs.append(_dot(vt_ref[n, rows[hd], :], a.astype(BF16)))
        tails.append(csum[0:1, :])

    def earlier_pair(i, carry):
        tails, accs = carry
        j0 = n - 1 - 2 * i
        j1 = j0 - 1
        j1_keys = jnp.maximum(j1, 0)
        j1_vals = jnp.where(j1 >= 0, j1, n_blocks)
        new_tails, new_accs = [], []
        for hd in heads:
            z0 = _dot_nt(_key_tile(k_ref, j0), qh[hd])
            z1 = _dot_nt(_key_tile(k_ref, j1_keys), qh[hd])
            c0 = _suffix_sum(tri, _neg_softplus(z0))
            c1 = _suffix_sum(tri, _neg_softplus(z1))
            tail1 = tails[hd] + c0[0:1, :]
            a0 = jnp.exp(z0 + c0 + tails[hd])
            a1 = jnp.exp(z1 + c1 + tail1)
            new_accs.append(accs[hd]
                            + _dot(vt_ref[j0, rows[hd], :], a0.astype(BF16))
                            + _dot(vt_ref[j1_vals, rows[hd], :], a1.astype(BF16)))
            new_tails.append(tail1 + c1[0:1, :])
        return tuple(new_tails), tuple(new_accs)

    _, accs = lax.fori_loop(0, (n + 1) // 2, earlier_pair, (tuple(tails), tuple(accs)))
    o_ref[0] = jnp.concatenate(accs, axis=0).T.astype(BF16)


def _sb_attention(q, k, v):
    n_blocks = q.shape[1] // ATT_TILE
    scratch = [pltpu.VMEM((n_blocks + 1, LANES, ATT_TILE), BF16)]
    return _attention_call(
        functools.partial(_sb_kernel, n_blocks=n_blocks),
        q, k, v, [], [], scratch, "stick_breaking_attention")


def _wo_norm_kernel(o_ref, h_ref, w_ref, g_ref, h_out_ref, y_ref):
    h = h_ref[...] + _dot(o_ref[...], w_ref[...])
    h_out_ref[...] = h
    y_ref[...] = _rms_norm(h, g_ref[...]).astype(BF16)


def _wo_norm(o, h, w_bf16, gain):
    m, d = h.shape
    row = pl.BlockSpec((ROW_TILE, d), lambda i: (i, 0))
    return pl.pallas_call(
        _wo_norm_kernel,
        grid=(m // ROW_TILE,),
        in_specs=[row, row,
                  pl.BlockSpec((d, d), lambda i: (0, 0)),
                  pl.BlockSpec((1, d), lambda i: (0, 0))],
        out_specs=[row, row],
        out_shape=[jax.ShapeDtypeStruct((m, d), F32),
                   jax.ShapeDtypeStruct((m, d), BF16)],
        compiler_params=_params(1),
        name="wo_norm",
    )(o, h, w_bf16, gain.reshape(1, d))


def _conv_ffn_kernel(y_ref, halo_ref, h_ref, wup_ref, cw_ref, wdn_ref, gn_ref, out_ref,
                     yext_ref, acc_ref, *, tiles_per_seq, n_chunks, final_norm):
    i = pl.program_id(0)
    seq_start = (i % tiles_per_seq) == 0
    halo = halo_ref[...]
    yext_ref[0:HALO, :] = jnp.where(seq_start, jnp.zeros_like(halo), halo)
    yext_ref[HALO:, :] = y_ref[...]
    acc_ref[...] = h_ref[...]
    tm = y_ref.shape[0]

    def conv(u, taps):
        out = taps[CONV_WIDTH:CONV_WIDTH + 1, :]
        for kk in range(CONV_WIDTH):
            lo = HALO - (CONV_WIDTH - 1) + kk
            out = out + taps[kk:kk + 1, :] * u[lo:lo + tm, :]
        return out

    def chunk(c, carry):
        yext = yext_ref[...]
        gate = conv(_dot(yext, wup_ref[c]), cw_ref[c])
        val = conv(_dot(yext, wup_ref[n_chunks + c]), cw_ref[n_chunks + c])
        act = gate * (1.0 / (1.0 + jnp.exp(-gate))) * val
        acc_ref[...] += _dot(act.astype(BF16), wdn_ref[c])
        return carry

    lax.fori_loop(0, n_chunks, chunk, 0)
    h = acc_ref[...]
    out_ref[...] = _rms_norm(h, gn_ref[...]) if final_norm else h


def _conv_ffn(y, h, wup_chunks, conv_taps, wdn_chunks, gain, final_norm, seq_len):
    m, d = h.shape
    n_chunks = wdn_chunks.shape[0]
    row = pl.BlockSpec((ROW_TILE, d), lambda i: (i, 0))
    halo = pl.BlockSpec((HALO, d), lambda i: (jnp.maximum(i * (ROW_TILE // HALO) - 1, 0), 0))
    whole = lambda a: pl.BlockSpec(a.shape, lambda i: (0,) * a.ndim)
    return pl.pallas_call(
        functools.partial(_conv_ffn_kernel, tiles_per_seq=seq_len // ROW_TILE,
                          n_chunks=n_chunks, final_norm=final_norm),
        grid=(m // ROW_TILE,),
        in_specs=[row, halo, row, whole(wup_chunks), whole(conv_taps), whole(wdn_chunks),
                  pl.BlockSpec((1, d), lambda i: (0, 0))],
        out_specs=row,
        out_shape=jax.ShapeDtypeStruct((m, d), F32),
        scratch_shapes=[pltpu.VMEM((ROW_TILE + HALO, d), BF16),
                        pltpu.VMEM((ROW_TILE, d), F32)],
        compiler_params=_params(1),
        name="conv_ffn",
    )(y, y, h, wup_chunks, conv_taps, wdn_chunks, gain.reshape(1, d))


def _chunk_columns(w, chunk):
    r, c = w.shape
    return w.reshape(r, c // chunk, chunk).transpose(1, 0, 2)


def kernel(x, attn_norm, w_qkv, w_o, rel_bias, ffn_norm, w_up, conv_w, conv_b, w_down, final_norm):
    b, s, d = x.shape
    depth = w_qkv.shape[0]
    f = w_down.shape[1]
    h = x.reshape(b * s, d)
    bias_table = _bias_table(rel_bias)
    for i in range(depth):
        q, k, v = _norm_qkv(h, attn_norm[i], w_qkv[i].astype(BF16))
        q, k, v = (t.reshape(b, s, d) for t in (q, k, v))
        if i % 2 == 0:
            o = _moba_attention(q, k, v, bias_table)
        else:
            o = _sb_attention(q, k, v)
        h, y = _wo_norm(o.reshape(b * s, d), h, w_o[i].astype(BF16), ffn_norm[i])
        taps = jnp.concatenate(
            [conv_w[i], conv_b[i][None, :],
             jnp.zeros((8 - CONV_WIDTH - 1, conv_w.shape[-1]), F32)], axis=0)
        last = i == depth - 1
        h = _conv_ffn(
            y, h,
            _chunk_columns(w_up[i].astype(BF16), FFN_CHUNK),
            _chunk_columns(taps, FFN_CHUNK),
            w_down[i].astype(BF16).reshape(f // FFN_CHUNK, FFN_CHUNK, d),
            final_norm, last, s)
    return h.reshape(b, s, d)
```

```python
import functools

import numpy as np

import jax
import jax.numpy as jnp
from jax import lax
from jax.experimental import pallas as pl
from jax.experimental.pallas import tpu as pltpu

F32 = jnp.float32
BF16 = jnp.bfloat16

N_HEADS = 16
HEAD_DIM = 64
HEADS_PER_STEP = 2
LANES = HEADS_PER_STEP * HEAD_DIM
MOBA_BLOCK = 256
MOBA_TOPK = 3
ATT_TILE = MOBA_BLOCK
CONV_WIDTH = 3
REL_BUCKETS = 32
NORM_EPS = 1e-6
NEG = -1e30
ROW_TILE = 512
FFN_CHUNK = 256
HALO = 16
VMEM_LIMIT = 56 * 1024 * 1024
HEADS = range(HEADS_PER_STEP)
HEAD_ROWS = [slice(hd * HEAD_DIM, (hd + 1) * HEAD_DIM) for hd in HEADS]

REL_BUCKET_START = (0, 1, 2, 3, 4, 5, 6, 7, 8, 9, 10, 11, 12, 13, 14, 15, 16,
                    19, 21, 24, 27, 31, 35, 40, 46, 52, 59, 67, 77, 87, 99, 113)


def _dot(a, b):
    return jnp.dot(a, b, preferred_element_type=F32)


def _dot_nt(a, b):
    return lax.dot_general(a, b, (((1,), (1,)), ((), ())), preferred_element_type=F32)


def _rms_norm(x, g):
    ms = jnp.mean(x * x, axis=-1, keepdims=True)
    return x * lax.rsqrt(ms + NORM_EPS) * g


def _split_bf16(x):
    hi = x.astype(BF16)
    lo = (x - hi.astype(F32)).astype(BF16)
    return hi, lo


def _params(n_axes):
    return pltpu.CompilerParams(
        dimension_semantics=("arbitrary",) * n_axes,
        vmem_limit_bytes=VMEM_LIMIT)


def _norm_qkv_kernel(x_ref, g_ref, w_ref, q_ref, k_ref, v_ref, *, d_model, q_scale):
    y = _rms_norm(x_ref[...], g_ref[...]).astype(BF16)
    for idx, out in enumerate((q_ref, k_ref, v_ref)):
        r = _dot(y, w_ref[:, idx * d_model:(idx + 1) * d_model])
        if idx == 0:
            r = r * q_scale
        out[...] = r.astype(BF16)


def _norm_qkv(h, gain, w_bf16):
    m, d = h.shape
    row = pl.BlockSpec((ROW_TILE, d), lambda i: (i, 0))
    out = jax.ShapeDtypeStruct((m, d), BF16)
    return pl.pallas_call(
        functools.partial(_norm_qkv_kernel, d_model=d, q_scale=HEAD_DIM ** -0.5),
        grid=(m // ROW_TILE,),
        in_specs=[row,
                  pl.BlockSpec((1, d), lambda i: (0, 0)),
                  pl.BlockSpec((d, 3 * d), lambda i: (0, 0))],
        out_specs=[row, row, row],
        out_shape=[out, out, out],
        compiler_params=_params(1),
        name="norm_qkv",
    )(h, gain.reshape(1, d), w_bf16)


def _bias_table_kernel(rb_ref, out_ref):
    h = pl.program_id(0)
    key = lax.broadcasted_iota(jnp.int32, (ATT_TILE, ATT_TILE), 0)
    qry = lax.broadcasted_iota(jnp.int32, (ATT_TILE, ATT_TILE), 1)
    for delta in range(3):
        dist = jnp.maximum(qry - key + ATT_TILE * delta, 0)
        val = jnp.full((ATT_TILE, ATT_TILE), rb_ref[h, 0], F32)
        for b in range(1, REL_BUCKETS):
            val = jnp.where(dist >= REL_BUCKET_START[b], rb_ref[h, b], val)
        if delta == 0:
            val = val + jnp.where(key <= qry, 0.0, NEG)
        out_ref[0, delta] = val


def _bias_table(rel_bias):
    return pl.pallas_call(
        _bias_table_kernel,
        grid=(N_HEADS,),
        in_specs=[pl.BlockSpec(memory_space=pltpu.SMEM)],
        out_specs=pl.BlockSpec((1, 3, ATT_TILE, ATT_TILE), lambda h: (h, 0, 0, 0)),
        out_shape=jax.ShapeDtypeStruct((N_HEADS, 3, ATT_TILE, ATT_TILE), F32),
        compiler_params=_params(1),
        name="rel_bias_table",
    )(rel_bias)


ITEM_QUERY, ITEM_KEY, ITEM_FIRST, ITEM_LAST = range(4)
RING = 4
SUBLANES = 8


def _item_table(n_blocks, n_stages):
    real = [(n, n - t, int(t == 0), int(t == n))
            for n in range(n_blocks) for t in range(n + 1)]
    steps = len(real) + n_stages - 1
    steps += -steps % RING
    lead = [(0, 0, 1, 0)] * (n_stages - 1)
    trail = [(0, 0, 1, 0)] * (steps + n_stages - 1 - len(lead) - len(real))
    return np.array(lead + real + trail, np.int32).T, steps


class _Item:
    def __init__(self, tab_ref, i, stage, n_stages):
        idx = i + (n_stages - 1) - stage
        self.query = tab_ref[ITEM_QUERY, idx]
        self.key = tab_ref[ITEM_KEY, idx]
        self.first = tab_ref[ITEM_FIRST, idx]
        self.last = tab_ref[ITEM_LAST, idx]


def _tile_rows(j):
    return pl.ds(pl.multiple_of(j * ATT_TILE, ATT_TILE), ATT_TILE)


def _grouped(x):
    return x.reshape(x.shape[0] // SUBLANES, SUBLANES, x.shape[1])


def _all_sublanes(x, op):
    for shift in (4, 2, 1):
        x = op(x, pltpu.roll(x, shift, 0))
    return x


def _store_head_queries(q_ref, qh_ref):
    q2 = q_ref[0]
    lane = lax.broadcasted_iota(jnp.int32, q2.shape, 1)
    for hd in HEADS:
        keep = (lane < HEAD_DIM) if hd == 0 else (lane >= HEAD_DIM)
        qh_ref[hd] = jnp.where(keep, q2, jnp.zeros_like(q2))


def _store_v_transposed(v_ref, vt_ref, n_blocks):
    for j in range(n_blocks):
        vj = v_ref[0, j * ATT_TILE:(j + 1) * ATT_TILE, :].astype(F32)
        vt_ref[j] = vj.T.astype(BF16)


def _scores(item, k_ref, qh_ref, hd):
    return _dot_nt(k_ref[0, _tile_rows(item.key), :], qh_ref[hd, _tile_rows(item.query), :])


def _first_grid_step():
    return (pl.program_id(0) == 0) & (pl.program_id(1) == 0)


def _run_pipeline(step, n_steps):
    def body(m, carry):
        finish = [step(m * RING + slot, slot) for slot in range(RING)]
        for write_out in finish:
            write_out()
        return carry

    lax.fori_loop(0, n_steps // RING, body, 0)


def _attention_call(kernel, n_stages, q, k, v, extra_inputs, extra_specs, scratch, name):
    b, s, d = q.shape
    n_blocks = s // ATT_TILE
    table, n_steps = _item_table(n_blocks, n_stages)
    seq_spec = pl.BlockSpec((1, s, LANES), lambda hp, bb: (bb, 0, hp))
    return pl.pallas_call(
        functools.partial(kernel, n_blocks=n_blocks, n_steps=n_steps),
        grid=(d // LANES, b),
        in_specs=[pl.BlockSpec(memory_space=pltpu.SMEM), seq_spec, seq_spec, seq_spec] + extra_specs,
        out_specs=seq_spec,
        out_shape=jax.ShapeDtypeStruct((b, s, d), BF16),
        scratch_shapes=[pltpu.VMEM((HEADS_PER_STEP, s, LANES), BF16),
                        pltpu.VMEM((n_blocks, LANES, ATT_TILE), BF16)] + scratch,
        compiler_params=_params(2),
        name=name,
    )(jnp.asarray(table), q, k, v, *extra_inputs)


def _tile_ring(dtype):
    return pltpu.VMEM((RING, HEADS_PER_STEP, ATT_TILE, ATT_TILE), dtype)


def _row_ring():
    return pltpu.VMEM((RING, HEADS_PER_STEP, SUBLANES, ATT_TILE), F32)


def _acc_ring():
    return pltpu.VMEM((RING, HEADS_PER_STEP, HEAD_DIM, ATT_TILE), F32)


def _zero(*refs):
    for ref in refs:
        ref[...] = jnp.zeros(ref.shape, ref.dtype)


MOBA_STAGES = 3


def _moba_block_choice(k_ref, qh_ref, sel_ref, n_blocks, seq_len):
    means = []
    for j in range(n_blocks):
        kj = k_ref[0, j * ATT_TILE:(j + 1) * ATT_TILE, :].astype(F32)
        means.append(jnp.sum(kj, axis=0, keepdims=True) * (1.0 / ATT_TILE))
    means.append(jnp.zeros((16 - n_blocks, LANES), F32))
    km_hi, km_lo = _split_bf16(jnp.concatenate(means, axis=0))

    blk = lax.broadcasted_iota(jnp.int32, (n_blocks, seq_len), 0)
    own = lax.shift_right_logical(lax.broadcasted_iota(jnp.int32, (n_blocks, seq_len), 1),
                                  ATT_TILE.bit_length() - 1)
    own_row = own[0:1, :]
    for hd in HEADS:
        qh = qh_ref[hd]
        gate = (_dot_nt(km_hi, qh) + _dot_nt(km_lo, qh))[0:n_blocks]
        gate = jnp.where(blk < own, gate, NEG)
        for j in range(n_blocks):
            gj = gate[j:j + 1, :]
            beats = (gate > gj) | ((gate == gj) & (blk < j))
            rank = jnp.sum(beats.astype(jnp.int32), axis=0, keepdims=True)
            attended = ((rank < MOBA_TOPK) & (own_row > j)) | (own_row == j)
            row = jnp.broadcast_to(jnp.where(attended, 0.0, NEG), (SUBLANES, seq_len))
            for n in range(n_blocks):
                sel_ref[hd, j, n] = row[:, n * ATT_TILE:(n + 1) * ATT_TILE]


def _moba_kernel(tab_ref, q_ref, k_ref, v_ref, bias_ref, o_ref,
                 qh_ref, vt_ref, sel_ref, s_ring, p_ring, alpha_ring, max_ring,
                 denom_ring, done_denom_ring, acc_ring, *, n_blocks, n_steps):
    @pl.when(_first_grid_step())
    def _():
        _zero(s_ring, p_ring, alpha_ring, max_ring, denom_ring, done_denom_ring, acc_ring)

    _store_head_queries(q_ref, qh_ref)
    _store_v_transposed(v_ref, vt_ref, n_blocks)
    _moba_block_choice(k_ref, qh_ref, sel_ref, n_blocks, q_ref.shape[1])

    def step(i, slot):
        prev = (slot - 1) % RING

        it = _Item(tab_ref, i, 0, MOBA_STAGES)
        for hd in HEADS:
            s_ring[slot, hd] = _scores(it, k_ref, qh_ref, hd)

        it = _Item(tab_ref, i, 2, MOBA_STAGES)
        for hd in HEADS:
            acc = jnp.where(it.first == 1, 0.0, acc_ring[prev, hd])
            acc = (_grouped(acc) * alpha_ring[prev, hd][None]).reshape(acc.shape)
            acc_ring[slot, hd] = acc + _dot(vt_ref[it.key, HEAD_ROWS[hd], :], p_ring[prev, hd])
            done_denom_ring[slot, hd] = denom_ring[prev, hd]
        done, done_query = it.last, it.query

        it = _Item(tab_ref, i, 1, MOBA_STAGES)
        for hd in HEADS:
            logit = _grouped(s_ring[prev, hd] + bias_ref[hd, jnp.minimum(it.query - it.key, 2)])
            logit = logit + sel_ref[hd, it.key, it.query][None]
            m_old = jnp.where(it.first == 1, NEG, max_ring[prev, hd])
            m_new = jnp.maximum(m_old, _all_sublanes(jnp.max(logit, axis=0), jnp.maximum))
            alpha = jnp.exp(m_old - m_new)
            p = jnp.exp(logit - m_new[None])
            denom_old = jnp.where(it.first == 1, 0.0, denom_ring[prev, hd])
            max_ring[slot, hd] = m_new
            denom_ring[slot, hd] = alpha * denom_old + _all_sublanes(jnp.sum(p, axis=0), jnp.add)
            alpha_ring[slot, hd] = alpha
            p_ring[slot, hd] = p.reshape(ATT_TILE, ATT_TILE).astype(BF16)

        def write_out():
            @pl.when(done == 1)
            def _():
                out = [(_grouped(acc_ring[slot, hd]) / done_denom_ring[slot, hd][None])
                       .reshape(HEAD_DIM, ATT_TILE) for hd in HEADS]
                o_ref[0, _tile_rows(done_query), :] = jnp.concatenate(out, axis=0).T.astype(BF16)

        return write_out

    _run_pipeline(step, n_steps)


def _moba_attention(q, k, v, bias_table):
    n_blocks = q.shape[1] // ATT_TILE
    bias_spec = pl.BlockSpec((HEADS_PER_STEP, 3, ATT_TILE, ATT_TILE),
                             lambda hp, bb: (hp, 0, 0, 0))
    scratch = [pltpu.VMEM((HEADS_PER_STEP, n_blocks, n_blocks, SUBLANES, ATT_TILE), F32),
               _tile_ring(F32),
               _tile_ring(BF16),
               _row_ring(),
               _row_ring(),
               _row_ring(),
               _row_ring(),
               _acc_ring()]
    return _attention_call(_moba_kernel, MOBA_STAGES, q, k, v,
                           [bias_table], [bias_spec], scratch, "moba_attention")


SB_STAGES = 4


def _neg_softplus(z):
    neg_z = -z
    return jnp.minimum(neg_z, 0.0) - jnp.log(1.0 + jnp.exp(jnp.minimum(z, neg_z)))


def _sb_kernel(tab_ref, q_ref, k_ref, v_ref, o_ref,
               qh_ref, vt_ref, tri_ref, mask_ref, z_ring, zm_ring, split_ring, e_ring,
               sum_ring, tail_ring, acc_ring, *, n_blocks, n_steps):
    @pl.when(_first_grid_step())
    def _():
        key = lax.broadcasted_iota(jnp.int32, (ATT_TILE, ATT_TILE), 0)
        qry = lax.broadcasted_iota(jnp.int32, (ATT_TILE, ATT_TILE), 1)
        tri = (qry >= key).astype(BF16)
        tri_ref[...] = jnp.concatenate([tri, tri], axis=1)
        mask_ref[0] = jnp.zeros((ATT_TILE, ATT_TILE), F32)
        mask_ref[1] = jnp.where(key < qry, 0.0, NEG)
        _zero(z_ring, zm_ring, split_ring, e_ring, sum_ring, tail_ring, acc_ring)

    _store_head_queries(q_ref, qh_ref)
    _store_v_transposed(v_ref, vt_ref, n_blocks)

    def step(i, slot):
        prev = (slot - 1) % RING

        it = _Item(tab_ref, i, 0, SB_STAGES)
        for hd in HEADS:
            z_ring[slot, hd] = _scores(it, k_ref, qh_ref, hd)

        for hd in HEADS:
            csum = _dot(tri_ref[...], split_ring[prev, hd])
            e_ring[slot, hd] = zm_ring[prev, hd] + csum
            sum_ring[slot, hd] = jnp.broadcast_to(csum[0:1, :], (SUBLANES, ATT_TILE))

        it = _Item(tab_ref, i, 1, SB_STAGES)
        for hd in HEADS:
            z = z_ring[prev, hd] + mask_ref[it.first]
            zm_ring[slot, hd] = z
            hi, lo = _split_bf16(_neg_softplus(z))
            split_ring[slot, hd, 0:ATT_TILE, :] = hi
            split_ring[slot, hd, ATT_TILE:, :] = lo

        it = _Item(tab_ref, i, 3, SB_STAGES)
        for hd in HEADS:
            tail = jnp.where(it.first == 1, 0.0, tail_ring[prev, hd])
            a = jnp.exp(_grouped(e_ring[prev, hd]) + tail[None]).reshape(ATT_TILE, ATT_TILE)
            acc = jnp.where(it.first == 1, 0.0, acc_ring[prev, hd])
            acc_ring[slot, hd] = acc + _dot(vt_ref[it.key, HEAD_ROWS[hd], :], a.astype(BF16))
            tail_ring[slot, hd] = tail + sum_ring[prev, hd]
        done, done_query = it.last, it.query

        def write_out():
            @pl.when(done == 1)
            def _():
                out = jnp.concatenate([acc_ring[slot, hd] for hd in HEADS], axis=0)
                o_ref[0, _tile_rows(done_query), :] = out.T.astype(BF16)

        return write_out

    _run_pipeline(step, n_steps)


def _sb_attention(q, k, v):
    scratch = [pltpu.VMEM((ATT_TILE, 2 * ATT_TILE), BF16),
               pltpu.VMEM((2, ATT_TILE, ATT_TILE), F32),
               _tile_ring(F32),
               _tile_ring(F32),
               pltpu.VMEM((RING, HEADS_PER_STEP, 2 * ATT_TILE, ATT_TILE), BF16),
               _tile_ring(F32),
               _row_ring(),
               _row_ring(),
               _acc_ring()]
    return _attention_call(_sb_kernel, SB_STAGES, q, k, v, [], [], scratch,
                           "stick_breaking_attention")


def _wo_norm_kernel(o_ref, h_ref, w_ref, g_ref, h_out_ref, y_ref):
    h = h_ref[...] + _dot(o_ref[...], w_ref[...])
    h_out_ref[...] = h
    y_ref[...] = _rms_norm(h, g_ref[...]).astype(BF16)


def _wo_norm(o, h, w_bf16, gain):
    m, d = h.shape
    row = pl.BlockSpec((ROW_TILE, d), lambda i: (i, 0))
    return pl.pallas_call(
        _wo_norm_kernel,
        grid=(m // ROW_TILE,),
        in_specs=[row, row,
                  pl.BlockSpec((d, d), lambda i: (0, 0)),
                  pl.BlockSpec((1, d), lambda i: (0, 0))],
        out_specs=[row, row],
        out_shape=[jax.ShapeDtypeStruct((m, d), F32),
                   jax.ShapeDtypeStruct((m, d), BF16)],
        compiler_params=_params(1),
        name="wo_norm",
    )(o, h, w_bf16, gain.reshape(1, d))


def _conv_ffn_kernel(y_ref, halo_ref, h_ref, wup_ref, cw_ref, wdn_ref, gn_ref, out_ref,
                     yext_ref, acc_ref, *, tiles_per_seq, n_chunks, final_norm):
    i = pl.program_id(0)
    seq_start = (i % tiles_per_seq) == 0
    halo = halo_ref[...]
    yext_ref[0:HALO, :] = jnp.where(seq_start, jnp.zeros_like(halo), halo)
    yext_ref[HALO:, :] = y_ref[...]
    acc_ref[...] = h_ref[...]
    tm = y_ref.shape[0]

    def conv(u, taps):
        out = taps[CONV_WIDTH:CONV_WIDTH + 1, :]
        for kk in range(CONV_WIDTH):
            lo = HALO - (CONV_WIDTH - 1) + kk
            out = out + taps[kk:kk + 1, :] * u[lo:lo + tm, :]
        return out

    up, act = {}, {}
    for step in range(n_chunks + 2):
        c = step
        if c < n_chunks:
            yext = yext_ref[...]
            up[c] = (_dot(yext, wup_ref[c]), _dot(yext, wup_ref[n_chunks + c]))
        c = step - 2
        if 0 <= c < n_chunks:
            acc_ref[...] += _dot(act.pop(c), wdn_ref[c])
        c = step - 1
        if 0 <= c < n_chunks:
            u_gate, u_val = up.pop(c)
            gate = conv(u_gate, cw_ref[c])
            val = conv(u_val, cw_ref[n_chunks + c])
            act[c] = (gate * (1.0 / (1.0 + jnp.exp(-gate))) * val).astype(BF16)
    h = acc_ref[...]
    out_ref[...] = _rms_norm(h, gn_ref[...]) if final_norm else h


def _conv_ffn(y, h, wup_chunks, conv_taps, wdn_chunks, gain, final_norm, seq_len):
    m, d = h.shape
    n_chunks = wdn_chunks.shape[0]
    row = pl.BlockSpec((ROW_TILE, d), lambda i: (i, 0))
    halo = pl.BlockSpec((HALO, d), lambda i: (jnp.maximum(i * (ROW_TILE // HALO) - 1, 0), 0))
    whole = lambda a: pl.BlockSpec(a.shape, lambda i: (0,) * a.ndim)
    return pl.pallas_call(
        functools.partial(_conv_ffn_kernel, tiles_per_seq=seq_len // ROW_TILE,
                          n_chunks=n_chunks, final_norm=final_norm),
        grid=(m // ROW_TILE,),
        in_specs=[row, halo, row, whole(wup_chunks), whole(conv_taps), whole(wdn_chunks),
                  pl.BlockSpec((1, d), lambda i: (0, 0))],
        out_specs=row,
        out_shape=jax.ShapeDtypeStruct((m, d), F32),
        scratch_shapes=[pltpu.VMEM((ROW_TILE + HALO, d), BF16),
                        pltpu.VMEM((ROW_TILE, d), F32)],
        compiler_params=_params(1),
        name="conv_ffn",
    )(y, y, h, wup_chunks, conv_taps, wdn_chunks, gain.reshape(1, d))


def _chunk_columns(w, chunk):
    r, c = w.shape
    return w.reshape(r, c // chunk, chunk).transpose(1, 0, 2)


def kernel(x, attn_norm, w_qkv, w_o, rel_bias, ffn_norm, w_up, conv_w, conv_b, w_down, final_norm):
    b, s, d = x.shape
    depth = w_qkv.shape[0]
    f = w_down.shape[1]
    h = x.reshape(b * s, d)
    bias_table = _bias_table(rel_bias)
    for i in range(depth):
        q, k, v = _norm_qkv(h, attn_norm[i], w_qkv[i].astype(BF16))
        q, k, v = (t.reshape(b, s, d) for t in (q, k, v))
        if i % 2 == 0:
            o = _moba_attention(q, k, v, bias_table)
        else:
            o = _sb_attention(q, k, v)
        h, y = _wo_norm(o.reshape(b * s, d), h, w_o[i].astype(BF16), ffn_norm[i])
        taps = jnp.concatenate(
            [conv_w[i], conv_b[i][None, :],
             jnp.zeros((8 - CONV_WIDTH - 1, conv_w.shape[-1]), F32)], axis=0)
        last = i == depth - 1
        h = _conv_ffn(
            y, h,
            _chunk_columns(w_up[i].astype(BF16), FFN_CHUNK),
            _chunk_columns(taps, FFN_CHUNK),
            w_down[i].astype(BF16).reshape(f // FFN_CHUNK, FFN_CHUNK, d),
            final_norm, last, s)
    return h.reshape(b, s, d)
```

```python
import functools

import numpy as np

import jax
import jax.numpy as jnp
from jax import lax
from jax.experimental import pallas as pl
from jax.experimental.pallas import tpu as pltpu

F32 = jnp.float32
BF16 = jnp.bfloat16

N_HEADS = 16
HEAD_DIM = 64
HEADS_PER_STEP = 2
LANES = HEADS_PER_STEP * HEAD_DIM
MOBA_BLOCK = 256
MOBA_TOPK = 3
ATT_TILE = MOBA_BLOCK
CONV_WIDTH = 3
REL_BUCKETS = 32
NORM_EPS = 1e-6
NEG = -1e30
ROW_TILE = 512
FFN_CHUNK = 256
HALO = 16
VMEM_LIMIT = 56 * 1024 * 1024
HEADS = range(HEADS_PER_STEP)
HEAD_ROWS = [slice(hd * HEAD_DIM, (hd + 1) * HEAD_DIM) for hd in HEADS]

REL_BUCKET_START = (0, 1, 2, 3, 4, 5, 6, 7, 8, 9, 10, 11, 12, 13, 14, 15, 16,
                    19, 21, 24, 27, 31, 35, 40, 46, 52, 59, 67, 77, 87, 99, 113)


def _dot(a, b):
    return jnp.dot(a, b, preferred_element_type=F32)


def _dot_nt(a, b):
    return lax.dot_general(a, b, (((1,), (1,)), ((), ())), preferred_element_type=F32)


def _rms_norm(x, g):
    ms = jnp.mean(x * x, axis=-1, keepdims=True)
    return x * lax.rsqrt(ms + NORM_EPS) * g


def _split_bf16(x):
    hi = x.astype(BF16)
    lo = (x - hi.astype(F32)).astype(BF16)
    return hi, lo


def _params(n_axes):
    return pltpu.CompilerParams(
        dimension_semantics=("arbitrary",) * n_axes,
        vmem_limit_bytes=VMEM_LIMIT)


def _norm_qkv_kernel(x_ref, g_ref, w_ref, q_ref, k_ref, v_ref, *, d_model, q_scale):
    y = _rms_norm(x_ref[...], g_ref[...]).astype(BF16)
    for idx, out in enumerate((q_ref, k_ref, v_ref)):
        r = _dot(y, w_ref[:, idx * d_model:(idx + 1) * d_model])
        if idx == 0:
            r = r * q_scale
        out[...] = r.astype(BF16)


def _norm_qkv(h, gain, w_bf16):
    m, d = h.shape
    row = pl.BlockSpec((ROW_TILE, d), lambda i: (i, 0))
    out = jax.ShapeDtypeStruct((m, d), BF16)
    return pl.pallas_call(
        functools.partial(_norm_qkv_kernel, d_model=d, q_scale=HEAD_DIM ** -0.5),
        grid=(m // ROW_TILE,),
        in_specs=[row,
                  pl.BlockSpec((1, d), lambda i: (0, 0)),
                  pl.BlockSpec((d, 3 * d), lambda i: (0, 0))],
        out_specs=[row, row, row],
        out_shape=[out, out, out],
        compiler_params=_params(1),
        name="norm_qkv",
    )(h, gain.reshape(1, d), w_bf16)


def _bias_table_kernel(rb_ref, out_ref):
    h = pl.program_id(0)
    key = lax.broadcasted_iota(jnp.int32, (ATT_TILE, ATT_TILE), 0)
    qry = lax.broadcasted_iota(jnp.int32, (ATT_TILE, ATT_TILE), 1)
    for delta in range(3):
        dist = jnp.maximum(qry - key + ATT_TILE * delta, 0)
        val = jnp.full((ATT_TILE, ATT_TILE), rb_ref[h, 0], F32)
        for b in range(1, REL_BUCKETS):
            val = jnp.where(dist >= REL_BUCKET_START[b], rb_ref[h, b], val)
        if delta == 0:
            val = val + jnp.where(key <= qry, 0.0, NEG)
        out_ref[0, delta] = val


def _bias_table(rel_bias):
    return pl.pallas_call(
        _bias_table_kernel,
        grid=(N_HEADS,),
        in_specs=[pl.BlockSpec(memory_space=pltpu.SMEM)],
        out_specs=pl.BlockSpec((1, 3, ATT_TILE, ATT_TILE), lambda h: (h, 0, 0, 0)),
        out_shape=jax.ShapeDtypeStruct((N_HEADS, 3, ATT_TILE, ATT_TILE), F32),
        compiler_params=_params(1),
        name="rel_bias_table",
    )(rel_bias)


ITEM_QUERY, ITEM_KEY, ITEM_FIRST, ITEM_LAST = range(4)
RING = 4
SUBLANES = 8


def _item_table(n_blocks, n_stages):
    real = [(n, n - t, int(t == 0), int(t == n))
            for n in range(n_blocks) for t in range(n + 1)]
    steps = len(real) + n_stages - 1
    steps += -steps % RING
    lead = [(0, 0, 1, 0)] * (n_stages - 1)
    trail = [(0, 0, 1, 0)] * (steps + n_stages - 1 - len(lead) - len(real))
    return np.array(lead + real + trail, np.int32).T, steps


class _Item:
    def __init__(self, tab_ref, i, stage, n_stages):
        idx = i + (n_stages - 1) - stage
        self.query = tab_ref[ITEM_QUERY, idx]
        self.key = tab_ref[ITEM_KEY, idx]
        self.first = tab_ref[ITEM_FIRST, idx]
        self.last = tab_ref[ITEM_LAST, idx]


def _tile_rows(j):
    return pl.ds(pl.multiple_of(j * ATT_TILE, ATT_TILE), ATT_TILE)


def _grouped(x):
    return x.reshape(x.shape[0] // SUBLANES, SUBLANES, x.shape[1])


def _all_sublanes(x, op):
    for shift in (4, 2, 1):
        x = op(x, pltpu.roll(x, shift, 0))
    return x


def _store_head_queries(q_ref, qh_ref):
    q2 = q_ref[0]
    lane = lax.broadcasted_iota(jnp.int32, q2.shape, 1)
    for hd in HEADS:
        keep = (lane < HEAD_DIM) if hd == 0 else (lane >= HEAD_DIM)
        qh_ref[hd] = jnp.where(keep, q2, jnp.zeros_like(q2))


def _store_v_transposed(v_ref, vt_ref, n_blocks):
    for j in range(n_blocks):
        vj = v_ref[0, j * ATT_TILE:(j + 1) * ATT_TILE, :].astype(F32)
        vt_ref[j] = vj.T.astype(BF16)


def _scores(item, k_ref, qh_ref, hd):
    return _dot_nt(k_ref[0, _tile_rows(item.key), :], qh_ref[hd, _tile_rows(item.query), :])


def _first_grid_step():
    return (pl.program_id(0) == 0) & (pl.program_id(1) == 0)


def _run_pipeline(step, n_steps):
    def body(m, carry):
        finish = [step(m * RING + slot, slot) for slot in range(RING)]
        for write_out in finish:
            write_out()
        return carry

    lax.fori_loop(0, n_steps // RING, body, 0)


def _attention_call(kernel, n_stages, q, k, v, extra_inputs, extra_specs, scratch, name):
    b, s, d = q.shape
    n_blocks = s // ATT_TILE
    table, n_steps = _item_table(n_blocks, n_stages)
    seq_spec = pl.BlockSpec((1, s, LANES), lambda hp, bb: (bb, 0, hp))
    return pl.pallas_call(
        functools.partial(kernel, n_blocks=n_blocks, n_steps=n_steps),
        grid=(d // LANES, b),
        in_specs=[pl.BlockSpec(memory_space=pltpu.SMEM), seq_spec, seq_spec, seq_spec] + extra_specs,
        out_specs=seq_spec,
        out_shape=jax.ShapeDtypeStruct((b, s, d), BF16),
        scratch_shapes=[pltpu.VMEM((HEADS_PER_STEP, s, LANES), BF16),
                        pltpu.VMEM((n_blocks, LANES, ATT_TILE), BF16)] + scratch,
        compiler_params=_params(2),
        name=name,
    )(jnp.asarray(table), q, k, v, *extra_inputs)


def _tile_ring(dtype):
    return pltpu.VMEM((RING, HEADS_PER_STEP, ATT_TILE, ATT_TILE), dtype)


def _row_ring():
    return pltpu.VMEM((RING, HEADS_PER_STEP, SUBLANES, ATT_TILE), F32)


def _acc_ring():
    return pltpu.VMEM((RING, HEADS_PER_STEP, HEAD_DIM, ATT_TILE), F32)


def _zero(*refs):
    for ref in refs:
        ref[...] = jnp.zeros(ref.shape, ref.dtype)


MOBA_STAGES = 3


def _moba_block_choice(k_ref, qh_ref, sel_ref, n_blocks, seq_len):
    means = []
    for j in range(n_blocks):
        kj = k_ref[0, j * ATT_TILE:(j + 1) * ATT_TILE, :].astype(F32)
        means.append(jnp.sum(kj, axis=0, keepdims=True) * (1.0 / ATT_TILE))
    means.append(jnp.zeros((16 - n_blocks, LANES), F32))
    km_hi, km_lo = _split_bf16(jnp.concatenate(means, axis=0))

    blk = lax.broadcasted_iota(jnp.int32, (n_blocks, seq_len), 0)
    own = lax.shift_right_logical(lax.broadcasted_iota(jnp.int32, (n_blocks, seq_len), 1),
                                  ATT_TILE.bit_length() - 1)
    own_row = own[0:1, :]
    for hd in HEADS:
        qh = qh_ref[hd]
        gate = (_dot_nt(km_hi, qh) + _dot_nt(km_lo, qh))[0:n_blocks]
        gate = jnp.where(blk < own, gate, NEG)
        for j in range(n_blocks):
            gj = gate[j:j + 1, :]
            beats = (gate > gj) | ((gate == gj) & (blk < j))
            rank = jnp.sum(beats.astype(jnp.int32), axis=0, keepdims=True)
            attended = ((rank < MOBA_TOPK) & (own_row > j)) | (own_row == j)
            row = jnp.broadcast_to(jnp.where(attended, 0.0, NEG), (SUBLANES, seq_len))
            for n in range(n_blocks):
                sel_ref[hd, j, n] = row[:, n * ATT_TILE:(n + 1) * ATT_TILE]


def _moba_kernel(tab_ref, q_ref, k_ref, v_ref, bias_ref, o_ref,
                 qh_ref, vt_ref, sel_ref, s_ring, p_ring, alpha_ring, max_ring,
                 denom_ring, done_denom_ring, acc_ring, *, n_blocks, n_steps):
    @pl.when(_first_grid_step())
    def _():
        _zero(s_ring, p_ring, alpha_ring, max_ring, denom_ring, done_denom_ring, acc_ring)

    _store_head_queries(q_ref, qh_ref)
    _store_v_transposed(v_ref, vt_ref, n_blocks)
    _moba_block_choice(k_ref, qh_ref, sel_ref, n_blocks, q_ref.shape[1])

    def step(i, slot):
        prev = (slot - 1) % RING

        it = _Item(tab_ref, i, 0, MOBA_STAGES)
        for hd in HEADS:
            s_ring[slot, hd] = _scores(it, k_ref, qh_ref, hd)

        it = _Item(tab_ref, i, 2, MOBA_STAGES)
        for hd in HEADS:
            acc = jnp.where(it.first == 1, 0.0, acc_ring[prev, hd])
            acc = (_grouped(acc) * alpha_ring[prev, hd][None]).reshape(acc.shape)
            acc_ring[slot, hd] = acc + _dot(vt_ref[it.key, HEAD_ROWS[hd], :], p_ring[prev, hd])
            done_denom_ring[slot, hd] = denom_ring[prev, hd]
        done, done_query = it.last, it.query

        it = _Item(tab_ref, i, 1, MOBA_STAGES)
        for hd in HEADS:
            logit = _grouped(s_ring[prev, hd] + bias_ref[hd, jnp.minimum(it.query - it.key, 2)])
            sel = sel_ref[hd, it.key, it.query]
            m_old = jnp.where(it.first == 1, NEG, max_ring[prev, hd])
            m_tile = _all_sublanes(jnp.max(logit, axis=0), jnp.maximum) + sel
            m_new = jnp.maximum(m_old, m_tile)
            alpha = jnp.exp(m_old - m_new)
            p = jnp.exp(logit - (m_new - sel)[None])
            denom_old = jnp.where(it.first == 1, 0.0, denom_ring[prev, hd])
            max_ring[slot, hd] = m_new
            denom_ring[slot, hd] = alpha * denom_old + _all_sublanes(jnp.sum(p, axis=0), jnp.add)
            alpha_ring[slot, hd] = alpha
            p_ring[slot, hd] = p.reshape(ATT_TILE, ATT_TILE).astype(BF16)

        def write_out():
            @pl.when(done == 1)
            def _():
                out = [(_grouped(acc_ring[slot, hd]) / done_denom_ring[slot, hd][None])
                       .reshape(HEAD_DIM, ATT_TILE) for hd in HEADS]
                o_ref[0, _tile_rows(done_query), :] = jnp.concatenate(out, axis=0).T.astype(BF16)

        return write_out

    _run_pipeline(step, n_steps)


def _moba_attention(q, k, v, bias_table):
    n_blocks = q.shape[1] // ATT_TILE
    bias_spec = pl.BlockSpec((HEADS_PER_STEP, 3, ATT_TILE, ATT_TILE),
                             lambda hp, bb: (hp, 0, 0, 0))
    scratch = [pltpu.VMEM((HEADS_PER_STEP, n_blocks, n_blocks, SUBLANES, ATT_TILE), F32),
               _tile_ring(F32),
               _tile_ring(BF16),
               _row_ring(),
               _row_ring(),
               _row_ring(),
               _row_ring(),
               _acc_ring()]
    return _attention_call(_moba_kernel, MOBA_STAGES, q, k, v,
                           [bias_table], [bias_spec], scratch, "moba_attention")


SB_STAGES = 4


def _neg_softplus(z):
    neg_z = -z
    return jnp.minimum(neg_z, 0.0) - jnp.log(1.0 + jnp.exp(jnp.minimum(z, neg_z)))


def _sb_kernel(tab_ref, q_ref, k_ref, v_ref, o_ref,
               qh_ref, vt_ref, tri_ref, mask_ref, z_ring, zm_ring, split_ring, e_ring,
               sum_ring, tail_ring, acc_ring, *, n_blocks, n_steps):
    @pl.when(_first_grid_step())
    def _():
        key = lax.broadcasted_iota(jnp.int32, (ATT_TILE, ATT_TILE), 0)
        qry = lax.broadcasted_iota(jnp.int32, (ATT_TILE, ATT_TILE), 1)
        tri = (qry >= key).astype(BF16)
        tri_ref[...] = jnp.concatenate([tri, tri], axis=1)
        mask_ref[0] = jnp.zeros((ATT_TILE, ATT_TILE), F32)
        mask_ref[1] = jnp.where(key < qry, 0.0, NEG)
        _zero(z_ring, zm_ring, split_ring, e_ring, sum_ring, tail_ring, acc_ring)

    _store_head_queries(q_ref, qh_ref)
    _store_v_transposed(v_ref, vt_ref, n_blocks)

    def step(i, slot):
        prev = (slot - 1) % RING

        it = _Item(tab_ref, i, 1, SB_STAGES)
        for hd in HEADS:
            z = z_ring[prev, hd] + mask_ref[it.first]
            zm_ring[slot, hd] = z
            hi, lo = _split_bf16(_neg_softplus(z))
            split_ring[slot, hd, 0:ATT_TILE, :] = hi
            split_ring[slot, hd, ATT_TILE:, :] = lo

        it = _Item(tab_ref, i, 0, SB_STAGES)
        for hd in HEADS:
            z_ring[slot, hd] = _scores(it, k_ref, qh_ref, hd)

        for hd in HEADS:
            csum = _dot(tri_ref[...], split_ring[prev, hd])
            e_ring[slot, hd] = zm_ring[prev, hd] + csum
            sum_ring[slot, hd] = jnp.broadcast_to(csum[0:1, :], (SUBLANES, ATT_TILE))

        it = _Item(tab_ref, i, 3, SB_STAGES)
        for hd in HEADS:
            tail = jnp.where(it.first == 1, 0.0, tail_ring[prev, hd])
            a = jnp.exp(_grouped(e_ring[prev, hd]) + tail[None]).reshape(ATT_TILE, ATT_TILE)
            acc = jnp.where(it.first == 1, 0.0, acc_ring[prev, hd])
            acc_ring[slot, hd] = acc + _dot(vt_ref[it.key, HEAD_ROWS[hd], :], a.astype(BF16))
            tail_ring[slot, hd] = tail + sum_ring[prev, hd]
        done, done_query = it.last, it.query

        def write_out():
            @pl.when(done == 1)
            def _():
                out = jnp.concatenate([acc_ring[slot, hd] for hd in HEADS], axis=0)
                o_ref[0, _tile_rows(done_query), :] = out.T.astype(BF16)

        return write_out

    _run_pipeline(step, n_steps)


def _sb_attention(q, k, v):
    scratch = [pltpu.VMEM((ATT_TILE, 2 * ATT_TILE), BF16),
               pltpu.VMEM((2, ATT_TILE, ATT_TILE), F32),
               _tile_ring(F32),
               _tile_ring(F32),
               pltpu.VMEM((RING, HEADS_PER_STEP, 2 * ATT_TILE, ATT_TILE), BF16),
               _tile_ring(F32),
               _row_ring(),
               _row_ring(),
               _acc_ring()]
    return _attention_call(_sb_kernel, SB_STAGES, q, k, v, [], [], scratch,
                           "stick_breaking_attention")


def _wo_conv_ffn_kernel(o_ref, o_halo_ref, h_ref, h_halo_ref, wo_ref, g_ref, wup_ref, cw_ref,
                        wdn_ref, gn_ref, out_ref, oext_ref, yext_ref, acc_ref,
                        *, tiles_per_seq, d_ff, final_norm):
    i = pl.program_id(0)
    seq_start = (i % tiles_per_seq) == 0
    tm = o_ref.shape[0]

    oext_ref[0:HALO, :] = o_halo_ref[...]
    oext_ref[HALO:, :] = o_ref[...]
    proj = _dot(oext_ref[...], wo_ref[...])
    h_halo = h_halo_ref[...] + proj[0:HALO]
    h_tile = h_ref[...] + proj[HALO:]
    y_halo = _rms_norm(h_halo, g_ref[...]).astype(BF16)
    yext_ref[0:HALO, :] = jnp.where(seq_start, jnp.zeros_like(y_halo), y_halo)
    yext_ref[HALO:, :] = _rms_norm(h_tile, g_ref[...]).astype(BF16)
    acc_ref[...] = h_tile

    def conv(u, taps):
        out = taps[CONV_WIDTH:CONV_WIDTH + 1, :]
        for kk in range(CONV_WIDTH):
            lo = HALO - (CONV_WIDTH - 1) + kk
            out = out + taps[kk:kk + 1, :] * u[lo:lo + tm, :]
        return out

    n_chunks = d_ff // FFN_CHUNK
    gate_cols = [slice(c * FFN_CHUNK, (c + 1) * FFN_CHUNK) for c in range(n_chunks)]
    val_cols = [slice(d_ff + c * FFN_CHUNK, d_ff + (c + 1) * FFN_CHUNK) for c in range(n_chunks)]
    up, act = {}, {}
    for step in range(n_chunks + 2):
        c = step
        if c < n_chunks:
            yext = yext_ref[...]
            up[c] = (_dot(yext, wup_ref[:, gate_cols[c]]), _dot(yext, wup_ref[:, val_cols[c]]))
        c = step - 2
        if 0 <= c < n_chunks:
            acc_ref[...] += _dot(act.pop(c), wdn_ref[gate_cols[c], :])
        c = step - 1
        if 0 <= c < n_chunks:
            u_gate, u_val = up.pop(c)
            gate = conv(u_gate, cw_ref[:, gate_cols[c]])
            val = conv(u_val, cw_ref[:, val_cols[c]])
            act[c] = (gate * (1.0 / (1.0 + jnp.exp(-gate))) * val).astype(BF16)
    h = acc_ref[...]
    out_ref[...] = _rms_norm(h, gn_ref[...]) if final_norm else h


def _wo_conv_ffn(o, h, wo, ffn_gain, wup, conv_taps, wdn, final_gain, final_norm, seq_len):
    m, d = h.shape
    d_ff = wdn.shape[0]
    row = pl.BlockSpec((ROW_TILE, d), lambda i: (i, 0))
    halo = pl.BlockSpec((HALO, d), lambda i: (jnp.maximum(i * (ROW_TILE // HALO) - 1, 0), 0))
    gain = pl.BlockSpec((1, d), lambda i: (0, 0))
    whole = lambda a: pl.BlockSpec(a.shape, lambda i: (0,) * a.ndim, pipeline_mode=pl.Buffered(1))
    return pl.pallas_call(
        functools.partial(_wo_conv_ffn_kernel, tiles_per_seq=seq_len // ROW_TILE,
                          d_ff=d_ff, final_norm=final_norm),
        grid=(m // ROW_TILE,),
        in_specs=[row, halo, row, halo, whole(wo), gain, whole(wup), whole(conv_taps), whole(wdn),
                  gain],
        out_specs=row,
        out_shape=jax.ShapeDtypeStruct((m, d), F32),
        scratch_shapes=[pltpu.VMEM((ROW_TILE + HALO, d), BF16),
                        pltpu.VMEM((ROW_TILE + HALO, d), BF16),
                        pltpu.VMEM((ROW_TILE, d), F32)],
        compiler_params=_params(1),
        name="wo_conv_ffn",
    )(o, o, h, h, wo, ffn_gain.reshape(1, d), wup, conv_taps, wdn, final_gain.reshape(1, d))


def kernel(x, attn_norm, w_qkv, w_o, rel_bias, ffn_norm, w_up, conv_w, conv_b, w_down, final_norm):
    b, s, d = x.shape
    depth = w_qkv.shape[0]
    h = x.reshape(b * s, d)
    bias_table = _bias_table(rel_bias)
    for i in range(depth):
        q, k, v = _norm_qkv(h, attn_norm[i], w_qkv[i].astype(BF16))
        q, k, v = (t.reshape(b, s, d) for t in (q, k, v))
        if i % 2 == 0:
            o = _moba_attention(q, k, v, bias_table)
        else:
            o = _sb_attention(q, k, v)
        taps = jnp.concatenate(
            [conv_w[i], conv_b[i][None, :],
             jnp.zeros((8 - CONV_WIDTH - 1, conv_w.shape[-1]), F32)], axis=0)
        h = _wo_conv_ffn(o.reshape(b * s, d), h, w_o[i].astype(BF16), ffn_norm[i],
                         w_up[i].astype(BF16), taps, w_down[i].astype(BF16),
                         final_norm, i == depth - 1, s)
    return h.reshape(b, s, d)
```

```python
import functools

import numpy as np

import jax
import jax.numpy as jnp
from jax import lax
from jax.experimental import pallas as pl
from jax.experimental.pallas import tpu as pltpu

F32 = jnp.float32
BF16 = jnp.bfloat16

N_HEADS = 16
HEAD_DIM = 64
HEADS_PER_STEP = 2
LANES = HEADS_PER_STEP * HEAD_DIM
MOBA_BLOCK = 256
MOBA_TOPK = 3
ATT_TILE = MOBA_BLOCK
CONV_WIDTH = 3
REL_BUCKETS = 32
NORM_EPS = 1e-6
NEG = -1e30
ROW_TILE = 512
FFN_CHUNK = 256
HALO = 16
VMEM_LIMIT = 56 * 1024 * 1024
HEADS = range(HEADS_PER_STEP)
HEAD_ROWS = [slice(hd * HEAD_DIM, (hd + 1) * HEAD_DIM) for hd in HEADS]

REL_BUCKET_START = (0, 1, 2, 3, 4, 5, 6, 7, 8, 9, 10, 11, 12, 13, 14, 15, 16,
                    19, 21, 24, 27, 31, 35, 40, 46, 52, 59, 67, 77, 87, 99, 113)


def _dot(a, b):
    return jnp.dot(a, b, preferred_element_type=F32)


def _dot_nt(a, b):
    return lax.dot_general(a, b, (((1,), (1,)), ((), ())), preferred_element_type=F32)


def _rms_norm(x, g):
    ms = jnp.mean(x * x, axis=-1, keepdims=True)
    return x * lax.rsqrt(ms + NORM_EPS) * g


def _split_bf16(x):
    hi = x.astype(BF16)
    lo = (x - hi.astype(F32)).astype(BF16)
    return hi, lo


def _params(n_axes):
    return pltpu.CompilerParams(
        dimension_semantics=("arbitrary",) * n_axes,
        vmem_limit_bytes=VMEM_LIMIT)


def _norm_qkv_kernel(x_ref, g_ref, w_ref, q_ref, k_ref, v_ref, *, d_model, q_scale):
    y = _rms_norm(x_ref[...], g_ref[...]).astype(BF16)
    for idx, out in enumerate((q_ref, k_ref, v_ref)):
        r = _dot(y, w_ref[:, idx * d_model:(idx + 1) * d_model])
        if idx == 0:
            r = r * q_scale
        out[...] = r.astype(BF16)


def _norm_qkv(h, gain, w_bf16):
    m, d = h.shape
    row = pl.BlockSpec((ROW_TILE, d), lambda i: (i, 0))
    out = jax.ShapeDtypeStruct((m, d), BF16)
    return pl.pallas_call(
        functools.partial(_norm_qkv_kernel, d_model=d, q_scale=HEAD_DIM ** -0.5),
        grid=(m // ROW_TILE,),
        in_specs=[row,
                  pl.BlockSpec((1, d), lambda i: (0, 0)),
                  pl.BlockSpec((d, 3 * d), lambda i: (0, 0))],
        out_specs=[row, row, row],
        out_shape=[out, out, out],
        compiler_params=_params(1),
        name="norm_qkv",
    )(h, gain.reshape(1, d), w_bf16)


def _bias_table_kernel(rb_ref, out_ref):
    h = pl.program_id(0)
    key = lax.broadcasted_iota(jnp.int32, (ATT_TILE, ATT_TILE), 0)
    qry = lax.broadcasted_iota(jnp.int32, (ATT_TILE, ATT_TILE), 1)
    for delta in range(3):
        dist = jnp.maximum(qry - key + ATT_TILE * delta, 0)
        val = jnp.full((ATT_TILE, ATT_TILE), rb_ref[h, 0], F32)
        for b in range(1, REL_BUCKETS):
            val = jnp.where(dist >= REL_BUCKET_START[b], rb_ref[h, b], val)
        if delta == 0:
            val = val + jnp.where(key <= qry, 0.0, NEG)
        out_ref[0, delta] = val


def _bias_table(rel_bias):
    return pl.pallas_call(
        _bias_table_kernel,
        grid=(N_HEADS,),
        in_specs=[pl.BlockSpec(memory_space=pltpu.SMEM)],
        out_specs=pl.BlockSpec((1, 3, ATT_TILE, ATT_TILE), lambda h: (h, 0, 0, 0)),
        out_shape=jax.ShapeDtypeStruct((N_HEADS, 3, ATT_TILE, ATT_TILE), F32),
        compiler_params=_params(1),
        name="rel_bias_table",
    )(rel_bias)


ITEM_QUERY, ITEM_KEY, ITEM_FIRST, ITEM_LAST = range(4)
ITEM_FIELDS = 4
IDLE_ITEM = (0, 0, 1, 0)
RING = 4
SUBLANES = 8


def _causal_items(n_blocks, nearest, farthest):
    items = []
    for n in range(n_blocks):
        ts = range(nearest, min(n, farthest) + 1)
        items += [(n, n - t, int(t == ts[0]), int(t == ts[-1])) for t in ts]
    return items


def _item_table(items, n_stages):
    steps = len(items) + n_stages - 1
    steps += -steps % RING
    lead = [IDLE_ITEM] * (n_stages - 1)
    trail = [IDLE_ITEM] * (steps + n_stages - 1 - len(lead) - len(items))
    return np.array(lead + list(items) + trail, np.int32).T, steps // RING


class _Item:
    def __init__(self, tab_ref, i, stage, n_stages):
        idx = i + (n_stages - 1) - stage
        self.query = tab_ref[ITEM_QUERY, idx]
        self.key = tab_ref[ITEM_KEY, idx]
        self.first = tab_ref[ITEM_FIRST, idx]
        self.last = tab_ref[ITEM_LAST, idx]


def _tile_rows(j):
    return pl.ds(pl.multiple_of(j * ATT_TILE, ATT_TILE), ATT_TILE)


def _grouped(x):
    return x.reshape(x.shape[0] // SUBLANES, SUBLANES, x.shape[1])


def _all_sublanes(x, op):
    for shift in (4, 2, 1):
        x = op(x, pltpu.roll(x, shift, 0))
    return x


def _store_head_queries(q_ref, qh_ref):
    q2 = q_ref[0]
    lane = lax.broadcasted_iota(jnp.int32, q2.shape, 1)
    for hd in HEADS:
        keep = (lane < HEAD_DIM) if hd == 0 else (lane >= HEAD_DIM)
        qh_ref[hd] = jnp.where(keep, q2, jnp.zeros_like(q2))


def _store_v_transposed(v_ref, vt_ref, n_blocks):
    for j in range(n_blocks):
        vj = v_ref[0, j * ATT_TILE:(j + 1) * ATT_TILE, :].astype(F32)
        vt_ref[j] = vj.T.astype(BF16)


def _scores(item, k_ref, qh_ref, hd):
    return _dot_nt(k_ref[0, _tile_rows(item.key), :], qh_ref[hd, _tile_rows(item.query), :])


def _first_grid_step():
    return (pl.program_id(0) == 0) & (pl.program_id(1) == 0)


def _run_pipeline(step, n_bodies):
    def body(m, carry):
        finish = [step(m * RING + slot, slot) for slot in range(RING)]
        for write_out in finish:
            write_out()
        return carry

    lax.fori_loop(0, n_bodies, body, 0)


def _attention_call(kernel, items, n_stages, q, k, v, extra_inputs, extra_specs, scratch, name):
    b, s, d = q.shape
    n_blocks = s // ATT_TILE
    table, n_bodies = _item_table(items, n_stages)
    seq_spec = pl.BlockSpec((1, s, LANES), lambda hp, bb: (bb, 0, hp))
    return pl.pallas_call(
        functools.partial(kernel, n_blocks=n_blocks, n_bodies=n_bodies),
        grid=(d // LANES, b),
        in_specs=[pl.BlockSpec(memory_space=pltpu.SMEM), seq_spec, seq_spec, seq_spec] + extra_specs,
        out_specs=seq_spec,
        out_shape=jax.ShapeDtypeStruct((b, s, d), BF16),
        scratch_shapes=[pltpu.VMEM((HEADS_PER_STEP, s, LANES), BF16),
                        pltpu.VMEM((n_blocks, LANES, ATT_TILE), BF16)] + scratch,
        compiler_params=_params(2),
        name=name,
    )(jnp.asarray(table), q, k, v, *extra_inputs)


def _tile_ring(dtype):
    return pltpu.VMEM((RING, HEADS_PER_STEP, ATT_TILE, ATT_TILE), dtype)


def _row_ring():
    return pltpu.VMEM((RING, HEADS_PER_STEP, SUBLANES, ATT_TILE), F32)


def _acc_ring():
    return pltpu.VMEM((RING, HEADS_PER_STEP, HEAD_DIM, ATT_TILE), F32)


def _zero(*refs):
    for ref in refs:
        ref[...] = jnp.zeros(ref.shape, ref.dtype)


MOBA_STAGES = 3


def _moba_block_choice(k_ref, qh_ref, sel_ref, n_blocks, seq_len):
    means = []
    for j in range(n_blocks):
        kj = k_ref[0, j * ATT_TILE:(j + 1) * ATT_TILE, :].astype(F32)
        means.append(jnp.sum(kj, axis=0, keepdims=True) * (1.0 / ATT_TILE))
    means.append(jnp.zeros((16 - n_blocks, LANES), F32))
    km_hi, km_lo = _split_bf16(jnp.concatenate(means, axis=0))

    blk = lax.broadcasted_iota(jnp.int32, (n_blocks, seq_len), 0)
    own = lax.shift_right_logical(lax.broadcasted_iota(jnp.int32, (n_blocks, seq_len), 1),
                                  ATT_TILE.bit_length() - 1)
    own_row = own[0:1, :]
    for hd in HEADS:
        qh = qh_ref[hd]
        gate = (_dot_nt(km_hi, qh) + _dot_nt(km_lo, qh))[0:n_blocks]
        gate = jnp.where(blk < own, gate, NEG)
        for j in range(n_blocks):
            gj = gate[j:j + 1, :]
            beats = (gate > gj) | ((gate == gj) & (blk < j))
            rank = jnp.sum(beats.astype(jnp.int32), axis=0, keepdims=True)
            attended = ((rank < MOBA_TOPK) & (own_row > j)) | (own_row == j)
            row = jnp.broadcast_to(jnp.where(attended, 0.0, NEG), (SUBLANES, seq_len))
            for n in range(n_blocks):
                sel_ref[hd, j, n] = row[:, n * ATT_TILE:(n + 1) * ATT_TILE]


def _moba_kernel(tab_ref, q_ref, k_ref, v_ref, bias_ref, o_ref,
                 qh_ref, vt_ref, sel_ref, s_ring, p_ring, alpha_ring, max_ring,
                 denom_ring, done_denom_ring, acc_ring, *, n_blocks, n_bodies):
    @pl.when(_first_grid_step())
    def _():
        _zero(s_ring, p_ring, alpha_ring, max_ring, denom_ring, done_denom_ring, acc_ring)

    _store_head_queries(q_ref, qh_ref)
    _store_v_transposed(v_ref, vt_ref, n_blocks)
    _moba_block_choice(k_ref, qh_ref, sel_ref, n_blocks, q_ref.shape[1])

    def step(i, slot):
        prev = (slot - 1) % RING

        it = _Item(tab_ref, i, 0, MOBA_STAGES)
        for hd in HEADS:
            s_ring[slot, hd] = _scores(it, k_ref, qh_ref, hd)

        it = _Item(tab_ref, i, 2, MOBA_STAGES)
        for hd in HEADS:
            acc = jnp.where(it.first == 1, 0.0, acc_ring[prev, hd])
            acc = (_grouped(acc) * alpha_ring[prev, hd][None]).reshape(acc.shape)
            acc_ring[slot, hd] = acc + _dot(vt_ref[it.key, HEAD_ROWS[hd], :], p_ring[prev, hd])
            done_denom_ring[slot, hd] = denom_ring[prev, hd]
        done, done_query = it.last, it.query

        it = _Item(tab_ref, i, 1, MOBA_STAGES)
        for hd in HEADS:
            logit = _grouped(s_ring[prev, hd] + bias_ref[hd, jnp.minimum(it.query - it.key, 2)])
            sel = sel_ref[hd, it.key, it.query]
            m_old = jnp.where(it.first == 1, NEG, max_ring[prev, hd])
            m_tile = _all_sublanes(jnp.max(logit, axis=0), jnp.maximum) + sel
            m_new = jnp.maximum(m_old, m_tile)
            alpha = jnp.exp(m_old - m_new)
            p = jnp.exp(logit - (m_new - sel)[None])
            denom_old = jnp.where(it.first == 1, 0.0, denom_ring[prev, hd])
            max_ring[slot, hd] = m_new
            denom_ring[slot, hd] = alpha * denom_old + _all_sublanes(jnp.sum(p, axis=0), jnp.add)
            alpha_ring[slot, hd] = alpha
            p_ring[slot, hd] = p.reshape(ATT_TILE, ATT_TILE).astype(BF16)

        def write_out():
            @pl.when(done == 1)
            def _():
                out = [(_grouped(acc_ring[slot, hd]) / done_denom_ring[slot, hd][None])
                       .reshape(HEAD_DIM, ATT_TILE) for hd in HEADS]
                o_ref[0, _tile_rows(done_query), :] = jnp.concatenate(out, axis=0).T.astype(BF16)

        return write_out

    _run_pipeline(step, n_bodies)


def _moba_attention(q, k, v, bias_table):
    n_blocks = q.shape[1] // ATT_TILE
    bias_spec = pl.BlockSpec((HEADS_PER_STEP, 3, ATT_TILE, ATT_TILE),
                             lambda hp, bb: (hp, 0, 0, 0))
    scratch = [pltpu.VMEM((HEADS_PER_STEP, n_blocks, n_blocks, SUBLANES, ATT_TILE), F32),
               _tile_ring(F32),
               _tile_ring(BF16),
               _row_ring(),
               _row_ring(),
               _row_ring(),
               _row_ring(),
               _acc_ring()]
    return _attention_call(_moba_kernel, _causal_items(n_blocks, 0, n_blocks), MOBA_STAGES,
                           q, k, v, [bias_table], [bias_spec], scratch, "moba_attention")


SB_STAGES = 4
SB_NEAR_TILES = 2
SB_SKIP_BELOW = -110.0


def _neg_softplus(z):
    neg_z = -z
    return jnp.minimum(neg_z, 0.0) - jnp.log(1.0 + jnp.exp(jnp.minimum(z, neg_z)))


def _sb_far_items(state_tail_ref, far_tab_ref, n_blocks):
    def put(idx, item):
        for field in range(ITEM_FIELDS):
            far_tab_ref[field, idx] = jnp.int32(item[field])

    lead = SB_STAGES - 1
    for r in range(lead):
        put(r, IDLE_ITEM)
    count = jnp.int32(lead)
    for n in range(SB_NEAR_TILES, n_blocks):
        items = [(n, n - t, int(t == SB_NEAR_TILES), int(t == n))
                 for t in range(SB_NEAR_TILES, n + 1)]
        live = jnp.maximum(jnp.max(state_tail_ref[n, 0]),
                           jnp.max(state_tail_ref[n, 1])) >= SB_SKIP_BELOW

        @pl.when(live)
        def _(items=items, count=count):
            for r, item in enumerate(items):
                put(count + r, item)

        count = count + jnp.where(live, len(items), 0)
    for r in range(lead + RING):
        put(count + r, IDLE_ITEM)
    n_items = count - lead
    return jnp.where(n_items > 0, (n_items + lead + RING - 1) // RING, 0)


def _sb_kernel(tab_ref, q_ref, k_ref, v_ref, o_ref,
               qh_ref, vt_ref, tri_ref, mask_ref, z_ring, zm_ring, split_ring, e_ring,
               sum_ring, tail_ring, acc_ring, state_tail_ref, state_acc_ref, far_tab_ref,
               *, n_blocks, n_bodies):
    @pl.when(_first_grid_step())
    def _():
        key = lax.broadcasted_iota(jnp.int32, (ATT_TILE, ATT_TILE), 0)
        qry = lax.broadcasted_iota(jnp.int32, (ATT_TILE, ATT_TILE), 1)
        tri = (qry >= key).astype(BF16)
        tri_ref[...] = jnp.concatenate([tri, tri], axis=1)
        mask_ref[0] = jnp.zeros((ATT_TILE, ATT_TILE), F32)
        mask_ref[1] = jnp.where(key < qry, 0.0, NEG)
        _zero(z_ring, zm_ring, split_ring, e_ring, sum_ring, tail_ring, acc_ring)

    _store_head_queries(q_ref, qh_ref)
    _store_v_transposed(v_ref, vt_ref, n_blocks)

    _zero(state_tail_ref, state_acc_ref)

    def pipeline_step(tab):
        def step(i, slot):
            prev = (slot - 1) % RING

            it = _Item(tab, i, 1, SB_STAGES)
            diagonal = (it.query == it.key).astype(jnp.int32)
            for hd in HEADS:
                z = z_ring[prev, hd] + mask_ref[diagonal]
                zm_ring[slot, hd] = z
                hi, lo = _split_bf16(_neg_softplus(z))
                split_ring[slot, hd, 0:ATT_TILE, :] = hi
                split_ring[slot, hd, ATT_TILE:, :] = lo

            it = _Item(tab, i, 0, SB_STAGES)
            for hd in HEADS:
                z_ring[slot, hd] = _scores(it, k_ref, qh_ref, hd)

            for hd in HEADS:
                csum = _dot(tri_ref[...], split_ring[prev, hd])
                e_ring[slot, hd] = zm_ring[prev, hd] + csum
                sum_ring[slot, hd] = jnp.broadcast_to(csum[0:1, :], (SUBLANES, ATT_TILE))

            it = _Item(tab, i, 3, SB_STAGES)
            for hd in HEADS:
                tail = jnp.where(it.first == 1, state_tail_ref[it.query, hd], tail_ring[prev, hd])
                a = jnp.exp(_grouped(e_ring[prev, hd]) + tail[None]).reshape(ATT_TILE, ATT_TILE)
                acc = jnp.where(it.first == 1, state_acc_ref[it.query, hd], acc_ring[prev, hd])
                acc_ring[slot, hd] = acc + _dot(vt_ref[it.key, HEAD_ROWS[hd], :], a.astype(BF16))
                tail_ring[slot, hd] = tail + sum_ring[prev, hd]
            done, done_query = it.last, it.query

            def write_out():
                @pl.when(done == 1)
                def _():
                    for hd in HEADS:
                        state_tail_ref[done_query, hd] = tail_ring[slot, hd]
                        state_acc_ref[done_query, hd] = acc_ring[slot, hd]
                    out = jnp.concatenate([acc_ring[slot, hd] for hd in HEADS], axis=0)
                    o_ref[0, _tile_rows(done_query), :] = out.T.astype(BF16)

            return write_out

        return step

    _run_pipeline(pipeline_step(tab_ref), n_bodies)
    _run_pipeline(pipeline_step(far_tab_ref), _sb_far_items(state_tail_ref, far_tab_ref, n_blocks))


def _sb_attention(q, k, v):
    n_blocks = q.shape[1] // ATT_TILE
    far_entries = (len(_causal_items(n_blocks, SB_NEAR_TILES, n_blocks))
                   + 2 * (SB_STAGES - 1) + RING)
    scratch = [pltpu.VMEM((ATT_TILE, 2 * ATT_TILE), BF16),
               pltpu.VMEM((2, ATT_TILE, ATT_TILE), F32),
               _tile_ring(F32),
               _tile_ring(F32),
               pltpu.VMEM((RING, HEADS_PER_STEP, 2 * ATT_TILE, ATT_TILE), BF16),
               _tile_ring(F32),
               _row_ring(),
               _row_ring(),
               _acc_ring(),
               pltpu.VMEM((n_blocks, HEADS_PER_STEP, SUBLANES, ATT_TILE), F32),
               pltpu.VMEM((n_blocks, HEADS_PER_STEP, HEAD_DIM, ATT_TILE), F32),
               pltpu.SMEM((ITEM_FIELDS, far_entries), jnp.int32)]
    return _attention_call(_sb_kernel, _causal_items(n_blocks, 0, SB_NEAR_TILES - 1), SB_STAGES,
                           q, k, v, [], [], scratch, "stick_breaking_attention")


def _wo_conv_ffn_kernel(o_ref, o_halo_ref, h_ref, h_halo_ref, wo_ref, g_ref, wup_ref, cw_ref,
                        wdn_ref, gn_ref, out_ref, oext_ref, yext_ref, acc_ref,
                        *, tiles_per_seq, d_ff, final_norm):
    i = pl.program_id(0)
    seq_start = (i % tiles_per_seq) == 0
    tm = o_ref.shape[0]

    oext_ref[0:HALO, :] = o_halo_ref[...]
    oext_ref[HALO:, :] = o_ref[...]
    proj = _dot(oext_ref[...], wo_ref[...])
    h_halo = h_halo_ref[...] + proj[0:HALO]
    h_tile = h_ref[...] + proj[HALO:]
    y_halo = _rms_norm(h_halo, g_ref[...]).astype(BF16)
    yext_ref[0:HALO, :] = jnp.where(seq_start, jnp.zeros_like(y_halo), y_halo)
    yext_ref[HALO:, :] = _rms_norm(h_tile, g_ref[...]).astype(BF16)
    acc_ref[...] = h_tile

    def conv(u, taps):
        out = taps[CONV_WIDTH:CONV_WIDTH + 1, :]
        for kk in range(CONV_WIDTH):
            lo = HALO - (CONV_WIDTH - 1) + kk
            out = out + taps[kk:kk + 1, :] * u[lo:lo + tm, :]
        return out

    n_chunks = d_ff // FFN_CHUNK
    gate_cols = [slice(c * FFN_CHUNK, (c + 1) * FFN_CHUNK) for c in range(n_chunks)]
    val_cols = [slice(d_ff + c * FFN_CHUNK, d_ff + (c + 1) * FFN_CHUNK) for c in range(n_chunks)]
    up, act = {}, {}
    for step in range(n_chunks + 2):
        c = step
        if c < n_chunks:
            yext = yext_ref[...]
            up[c] = (_dot(yext, wup_ref[:, gate_cols[c]]), _dot(yext, wup_ref[:, val_cols[c]]))
        c = step - 2
        if 0 <= c < n_chunks:
            acc_ref[...] += _dot(act.pop(c), wdn_ref[gate_cols[c], :])
        c = step - 1
        if 0 <= c < n_chunks:
            u_gate, u_val = up.pop(c)
            gate = conv(u_gate, cw_ref[:, gate_cols[c]])
            val = conv(u_val, cw_ref[:, val_cols[c]])
            act[c] = (gate * (1.0 / (1.0 + jnp.exp(-gate))) * val).astype(BF16)
    h = acc_ref[...]
    out_ref[...] = _rms_norm(h, gn_ref[...]) if final_norm else h


def _wo_conv_ffn(o, h, wo, ffn_gain, wup, conv_taps, wdn, final_gain, final_norm, seq_len):
    m, d = h.shape
    d_ff = wdn.shape[0]
    row = pl.BlockSpec((ROW_TILE, d), lambda i: (i, 0))
    halo = pl.BlockSpec((HALO, d), lambda i: (jnp.maximum(i * (ROW_TILE // HALO) - 1, 0), 0))
    gain = pl.BlockSpec((1, d), lambda i: (0, 0))
    whole = lambda a: pl.BlockSpec(a.shape, lambda i: (0,) * a.ndim, pipeline_mode=pl.Buffered(1))
    return pl.pallas_call(
        functools.partial(_wo_conv_ffn_kernel, tiles_per_seq=seq_len // ROW_TILE,
                          d_ff=d_ff, final_norm=final_norm),
        grid=(m // ROW_TILE,),
        in_specs=[row, halo, row, halo, whole(wo), gain, whole(wup), whole(conv_taps), whole(wdn),
                  gain],
        out_specs=row,
        out_shape=jax.ShapeDtypeStruct((m, d), F32),
        scratch_shapes=[pltpu.VMEM((ROW_TILE + HALO, d), BF16),
                        pltpu.VMEM((ROW_TILE + HALO, d), BF16),
                        pltpu.VMEM((ROW_TILE, d), F32)],
        compiler_params=_params(1),
        name="wo_conv_ffn",
    )(o, o, h, h, wo, ffn_gain.reshape(1, d), wup, conv_taps, wdn, final_gain.reshape(1, d))


def kernel(x, attn_norm, w_qkv, w_o, rel_bias, ffn_norm, w_up, conv_w, conv_b, w_down, final_norm):
    b, s, d = x.shape
    depth = w_qkv.shape[0]
    h = x.reshape(b * s, d)
    bias_table = _bias_table(rel_bias)
    for i in range(depth):
        q, k, v = _norm_qkv(h, attn_norm[i], w_qkv[i].astype(BF16))
        q, k, v = (t.reshape(b, s, d) for t in (q, k, v))
        if i % 2 == 0:
            o = _moba_attention(q, k, v, bias_table)
        else:
            o = _sb_attention(q, k, v)
        taps = jnp.concatenate(
            [conv_w[i], conv_b[i][None, :],
             jnp.zeros((8 - CONV_WIDTH - 1, conv_w.shape[-1]), F32)], axis=0)
        h = _wo_conv_ffn(o.reshape(b * s, d), h, w_o[i].astype(BF16), ffn_norm[i],
                         w_up[i].astype(BF16), taps, w_down[i].astype(BF16),
                         final_norm, i == depth - 1, s)
    return h.reshape(b, s, d)
```

```python
import functools

import numpy as np

import jax
import jax.numpy as jnp
from jax import lax
from jax.experimental import pallas as pl
from jax.experimental.pallas import tpu as pltpu

F32 = jnp.float32
BF16 = jnp.bfloat16

N_HEADS = 16
HEAD_DIM = 64
HEADS_PER_STEP = 2
LANES = HEADS_PER_STEP * HEAD_DIM
MOBA_BLOCK = 256
MOBA_TOPK = 3
ATT_TILE = MOBA_BLOCK
CONV_WIDTH = 3
REL_BUCKETS = 32
NORM_EPS = 1e-6
NEG = -1e30
ROW_TILE = 512
FFN_CHUNK = 256
HALO = 16
VMEM_LIMIT = 56 * 1024 * 1024
HEADS = range(HEADS_PER_STEP)
HEAD_ROWS = [slice(hd * HEAD_DIM, (hd + 1) * HEAD_DIM) for hd in HEADS]

REL_BUCKET_START = (0, 1, 2, 3, 4, 5, 6, 7, 8, 9, 10, 11, 12, 13, 14, 15, 16,
                    19, 21, 24, 27, 31, 35, 40, 46, 52, 59, 67, 77, 87, 99, 113)


def _dot(a, b):
    return jnp.dot(a, b, preferred_element_type=F32)


def _dot_nt(a, b):
    return lax.dot_general(a, b, (((1,), (1,)), ((), ())), preferred_element_type=F32)


def _rms_norm(x, g):
    ms = jnp.mean(x * x, axis=-1, keepdims=True)
    return x * lax.rsqrt(ms + NORM_EPS) * g


def _split_bf16(x):
    hi = x.astype(BF16)
    lo = (x - hi.astype(F32)).astype(BF16)
    return hi, lo


def _params(n_axes):
    return pltpu.CompilerParams(
        dimension_semantics=("arbitrary",) * n_axes,
        vmem_limit_bytes=VMEM_LIMIT)


def _norm_qkv_kernel(x_ref, g_ref, w_ref, q_ref, k_ref, v_ref, *, d_model, q_scale):
    y = _rms_norm(x_ref[...], g_ref[...]).astype(BF16)
    for idx, out in enumerate((q_ref, k_ref, v_ref)):
        r = _dot(y, w_ref[:, idx * d_model:(idx + 1) * d_model])
        if idx == 0:
            r = r * q_scale
        out[...] = r.astype(BF16)


def _norm_qkv(h, gain, w_bf16):
    m, d = h.shape
    row = pl.BlockSpec((ROW_TILE, d), lambda i: (i, 0))
    out = jax.ShapeDtypeStruct((m, d), BF16)
    return pl.pallas_call(
        functools.partial(_norm_qkv_kernel, d_model=d, q_scale=HEAD_DIM ** -0.5),
        grid=(m // ROW_TILE,),
        in_specs=[row,
                  pl.BlockSpec((1, d), lambda i: (0, 0)),
                  pl.BlockSpec((d, 3 * d), lambda i: (0, 0))],
        out_specs=[row, row, row],
        out_shape=[out, out, out],
        compiler_params=_params(1),
        name="norm_qkv",
    )(h, gain.reshape(1, d), w_bf16)


def _bias_table_kernel(rb_ref, out_ref):
    h = pl.program_id(0)
    key = lax.broadcasted_iota(jnp.int32, (ATT_TILE, ATT_TILE), 0)
    qry = lax.broadcasted_iota(jnp.int32, (ATT_TILE, ATT_TILE), 1)
    for delta in range(3):
        dist = jnp.maximum(qry - key + ATT_TILE * delta, 0)
        val = jnp.full((ATT_TILE, ATT_TILE), rb_ref[h, 0], F32)
        for b in range(1, REL_BUCKETS):
            val = jnp.where(dist >= REL_BUCKET_START[b], rb_ref[h, b], val)
        if delta == 0:
            val = val + jnp.where(key <= qry, 0.0, NEG)
        out_ref[0, delta] = val


def _bias_table(rel_bias):
    return pl.pallas_call(
        _bias_table_kernel,
        grid=(N_HEADS,),
        in_specs=[pl.BlockSpec(memory_space=pltpu.SMEM)],
        out_specs=pl.BlockSpec((1, 3, ATT_TILE, ATT_TILE), lambda h: (h, 0, 0, 0)),
        out_shape=jax.ShapeDtypeStruct((N_HEADS, 3, ATT_TILE, ATT_TILE), F32),
        compiler_params=_params(1),
        name="rel_bias_table",
    )(rel_bias)


ITEM_QUERY, ITEM_KEY, ITEM_FIRST, ITEM_LAST = range(4)
ITEM_FIELDS = 4
IDLE_ITEM = (0, 0, 1, 0)
RING = 4
SEQS_PER_STEP = 4
SUBLANES = 8


def _causal_items(n_blocks, nearest, farthest):
    items = []
    for tile in range(SEQS_PER_STEP * n_blocks):
        ts = range(nearest, min(tile % n_blocks, farthest) + 1)
        items += [(tile, tile - t, int(t == ts[0]), int(t == ts[-1])) for t in ts]
    return items


def _item_table(items, n_stages):
    steps = len(items) + n_stages - 1
    steps += -steps % RING
    lead = [IDLE_ITEM] * (n_stages - 1)
    trail = [IDLE_ITEM] * (steps + n_stages - 1 - len(lead) - len(items))
    return np.array(lead + list(items) + trail, np.int32).T, steps // RING


class _Item:
    def __init__(self, tab_ref, i, stage, n_stages):
        idx = i + (n_stages - 1) - stage
        self.query = tab_ref[ITEM_QUERY, idx]
        self.key = tab_ref[ITEM_KEY, idx]
        self.first = tab_ref[ITEM_FIRST, idx]
        self.last = tab_ref[ITEM_LAST, idx]


def _tile_rows(j):
    return pl.ds(pl.multiple_of(j * ATT_TILE, ATT_TILE), ATT_TILE)


def _grouped(x):
    return x.reshape(x.shape[0] // SUBLANES, SUBLANES, x.shape[1])


def _all_sublanes(x, op):
    for shift in (4, 2, 1):
        x = op(x, pltpu.roll(x, shift, 0))
    return x


def _store_head_queries(q_ref, qh_ref):
    q2 = q_ref[...]
    lane = lax.broadcasted_iota(jnp.int32, q2.shape, 1)
    for hd in HEADS:
        keep = (lane < HEAD_DIM) if hd == 0 else (lane >= HEAD_DIM)
        qh_ref[hd] = jnp.where(keep, q2, jnp.zeros_like(q2))


def _store_v_transposed(v_ref, vt_ref):
    for j in range(vt_ref.shape[0]):
        vj = v_ref[j * ATT_TILE:(j + 1) * ATT_TILE, :].astype(F32)
        vt_ref[j] = vj.T.astype(BF16)


def _scores(item, k_ref, qh_ref, hd):
    return _dot_nt(k_ref[_tile_rows(item.key), :], qh_ref[hd, _tile_rows(item.query), :])


def _first_grid_step():
    return (pl.program_id(0) == 0) & (pl.program_id(1) == 0)


def _run_pipeline(step, n_bodies):
    def body(m, carry):
        finish = [step(m * RING + slot, slot) for slot in range(RING)]
        for write_out in finish:
            write_out()
        return carry

    lax.fori_loop(0, n_bodies, body, 0)


def _attention_call(kernel, items, n_stages, seq_len, q, k, v, extra_inputs, extra_specs,
                    scratch, name):
    m, d = q.shape
    rows = SEQS_PER_STEP * seq_len
    n_blocks = seq_len // ATT_TILE
    table, n_bodies = _item_table(items, n_stages)
    seq_spec = pl.BlockSpec((rows, LANES), lambda hp, g: (g, hp))
    return pl.pallas_call(
        functools.partial(kernel, n_blocks=n_blocks, n_bodies=n_bodies),
        grid=(d // LANES, m // rows),
        in_specs=[pl.BlockSpec(memory_space=pltpu.SMEM), seq_spec, seq_spec, seq_spec] + extra_specs,
        out_specs=seq_spec,
        out_shape=jax.ShapeDtypeStruct((m, d), BF16),
        scratch_shapes=[pltpu.VMEM((HEADS_PER_STEP, rows, LANES), BF16),
                        pltpu.VMEM((SEQS_PER_STEP * n_blocks, LANES, ATT_TILE), BF16)]
        + scratch,
        compiler_params=_params(2),
        name=name,
    )(jnp.asarray(table), q, k, v, *extra_inputs)


def _tile_ring(dtype):
    return pltpu.VMEM((RING, HEADS_PER_STEP, ATT_TILE, ATT_TILE), dtype)


def _row_ring():
    return pltpu.VMEM((RING, HEADS_PER_STEP, SUBLANES, ATT_TILE), F32)


def _acc_ring():
    return pltpu.VMEM((RING, HEADS_PER_STEP, HEAD_DIM, ATT_TILE), F32)


def _zero(*refs):
    for ref in refs:
        ref[...] = jnp.zeros(ref.shape, ref.dtype)


MOBA_STAGES = 3


def _moba_block_choice(seq, k_ref, qh_ref, sel_ref, n_blocks):
    seq_len = n_blocks * ATT_TILE
    seq_rows = pl.ds(pl.multiple_of(seq * seq_len, seq_len), seq_len)
    means = []
    for j in range(n_blocks):
        kj = k_ref[_tile_rows(seq * n_blocks + j), :].astype(F32)
        means.append(jnp.sum(kj, axis=0, keepdims=True) * (1.0 / ATT_TILE))
    means.append(jnp.zeros((16 - n_blocks, LANES), F32))
    km_hi, km_lo = _split_bf16(jnp.concatenate(means, axis=0))

    blk = lax.broadcasted_iota(jnp.int32, (n_blocks, seq_len), 0)
    own = lax.shift_right_logical(lax.broadcasted_iota(jnp.int32, (n_blocks, seq_len), 1),
                                  ATT_TILE.bit_length() - 1)
    own_row = own[0:1, :]
    for hd in HEADS:
        qh = qh_ref[hd, seq_rows, :]
        gate = (_dot_nt(km_hi, qh) + _dot_nt(km_lo, qh))[0:n_blocks]
        gate = jnp.where(blk < own, gate, NEG)
        for j in range(n_blocks):
            gj = gate[j:j + 1, :]
            beats = (gate > gj) | ((gate == gj) & (blk < j))
            rank = jnp.sum(beats.astype(jnp.int32), axis=0, keepdims=True)
            attended = ((rank < MOBA_TOPK) & (own_row > j)) | (own_row == j)
            row = jnp.broadcast_to(jnp.where(attended, 0.0, NEG), (SUBLANES, seq_len))
            for n in range(n_blocks):
                sel_ref[hd, seq * n_blocks + j, n] = row[:, n * ATT_TILE:(n + 1) * ATT_TILE]


def _moba_kernel(tab_ref, q_ref, k_ref, v_ref, bias_ref, o_ref,
                 qh_ref, vt_ref, sel_ref, s_ring, p_ring, alpha_ring, max_ring,
                 denom_ring, done_denom_ring, acc_ring, *, n_blocks, n_bodies):
    @pl.when(_first_grid_step())
    def _():
        _zero(s_ring, p_ring, alpha_ring, max_ring, denom_ring, done_denom_ring, acc_ring)

    _store_head_queries(q_ref, qh_ref)
    _store_v_transposed(v_ref, vt_ref)

    def choose(seq, carry):
        _moba_block_choice(seq, k_ref, qh_ref, sel_ref, n_blocks)
        return carry

    lax.fori_loop(0, SEQS_PER_STEP, choose, 0)

    def step(i, slot):
        prev = (slot - 1) % RING

        it = _Item(tab_ref, i, 0, MOBA_STAGES)
        for hd in HEADS:
            s_ring[slot, hd] = _scores(it, k_ref, qh_ref, hd)

        it = _Item(tab_ref, i, 2, MOBA_STAGES)
        for hd in HEADS:
            acc = jnp.where(it.first == 1, 0.0, acc_ring[prev, hd])
            acc = (_grouped(acc) * alpha_ring[prev, hd][None]).reshape(acc.shape)
            acc_ring[slot, hd] = acc + _dot(vt_ref[it.key, HEAD_ROWS[hd], :], p_ring[prev, hd])
            done_denom_ring[slot, hd] = denom_ring[prev, hd]
        done, done_query = it.last, it.query

        it = _Item(tab_ref, i, 1, MOBA_STAGES)
        for hd in HEADS:
            logit = _grouped(s_ring[prev, hd] + bias_ref[hd, jnp.minimum(it.query - it.key, 2)])
            sel = sel_ref[hd, it.key, it.query % n_blocks]
            m_old = jnp.where(it.first == 1, NEG, max_ring[prev, hd])
            m_tile = _all_sublanes(jnp.max(logit, axis=0), jnp.maximum) + sel
            m_new = jnp.maximum(m_old, m_tile)
            alpha = jnp.exp(m_old - m_new)
            p = jnp.exp(logit - (m_new - sel)[None])
            denom_old = jnp.where(it.first == 1, 0.0, denom_ring[prev, hd])
            max_ring[slot, hd] = m_new
            denom_ring[slot, hd] = alpha * denom_old + _all_sublanes(jnp.sum(p, axis=0), jnp.add)
            alpha_ring[slot, hd] = alpha
            p_ring[slot, hd] = p.reshape(ATT_TILE, ATT_TILE).astype(BF16)

        def write_out():
            @pl.when(done == 1)
            def _():
                out = [(_grouped(acc_ring[slot, hd]) / done_denom_ring[slot, hd][None])
                       .reshape(HEAD_DIM, ATT_TILE) for hd in HEADS]
                o_ref[_tile_rows(done_query), :] = jnp.concatenate(out, axis=0).T.astype(BF16)

        return write_out

    _run_pipeline(step, n_bodies)


def _moba_attention(q, k, v, bias_table, seq_len):
    n_blocks = seq_len // ATT_TILE
    bias_spec = pl.BlockSpec((HEADS_PER_STEP, 3, ATT_TILE, ATT_TILE),
                             lambda hp, g: (hp, 0, 0, 0))
    scratch = [pltpu.VMEM((HEADS_PER_STEP, SEQS_PER_STEP * n_blocks, n_blocks, SUBLANES, ATT_TILE),
                          F32),
               _tile_ring(F32),
               _tile_ring(BF16),
               _row_ring(),
               _row_ring(),
               _row_ring(),
               _row_ring(),
               _acc_ring()]
    return _attention_call(_moba_kernel, _causal_items(n_blocks, 0, n_blocks), MOBA_STAGES,
                           seq_len, q, k, v, [bias_table], [bias_spec], scratch, "moba_attention")


SB_STAGES = 4
SB_NEAR_TILES = 2
SB_SKIP_BELOW = -110.0


def _neg_softplus(z):
    neg_z = -z
    return jnp.minimum(neg_z, 0.0) - jnp.log(1.0 + jnp.exp(jnp.minimum(z, neg_z)))


def _sb_far_items(state_tail_ref, far_tab_ref, n_blocks):
    def put(idx, item):
        for field in range(ITEM_FIELDS):
            far_tab_ref[field, idx] = jnp.int32(item[field])

    lead = SB_STAGES - 1
    for r in range(lead):
        put(r, IDLE_ITEM)
    count = jnp.int32(lead)
    for tile in range(SEQS_PER_STEP * n_blocks):
        n = tile % n_blocks
        if n < SB_NEAR_TILES:
            continue
        items = [(tile, tile - t, int(t == SB_NEAR_TILES), int(t == n))
                 for t in range(SB_NEAR_TILES, n + 1)]
        live = jnp.maximum(jnp.max(state_tail_ref[tile, 0]),
                           jnp.max(state_tail_ref[tile, 1])) >= SB_SKIP_BELOW

        @pl.when(live)
        def _(items=items, count=count):
            for r, item in enumerate(items):
                put(count + r, item)

        count = count + jnp.where(live, len(items), 0)
    for r in range(lead + RING):
        put(count + r, IDLE_ITEM)
    n_items = count - lead
    return jnp.where(n_items > 0, (n_items + lead + RING - 1) // RING, 0)


def _sb_kernel(tab_ref, q_ref, k_ref, v_ref, o_ref,
               qh_ref, vt_ref, tri_ref, mask_ref, z_ring, zm_ring, split_ring, e_ring,
               sum_ring, tail_ring, acc_ring, state_tail_ref, state_acc_ref, far_tab_ref,
               *, n_blocks, n_bodies):
    @pl.when(_first_grid_step())
    def _():
        key = lax.broadcasted_iota(jnp.int32, (ATT_TILE, ATT_TILE), 0)
        qry = lax.broadcasted_iota(jnp.int32, (ATT_TILE, ATT_TILE), 1)
        tri = (qry >= key).astype(BF16)
        tri_ref[...] = jnp.concatenate([tri, tri], axis=1)
        mask_ref[0] = jnp.zeros((ATT_TILE, ATT_TILE), F32)
        mask_ref[1] = jnp.where(key < qry, 0.0, NEG)
        _zero(z_ring, zm_ring, split_ring, e_ring, sum_ring, tail_ring, acc_ring)

    _store_head_queries(q_ref, qh_ref)
    _store_v_transposed(v_ref, vt_ref)
    _zero(state_tail_ref, state_acc_ref)

    def pipeline_step(tab):
        def step(i, slot):
            prev = (slot - 1) % RING

            it = _Item(tab, i, 1, SB_STAGES)
            diagonal = (it.query == it.key).astype(jnp.int32)
            for hd in HEADS:
                z = z_ring[prev, hd] + mask_ref[diagonal]
                zm_ring[slot, hd] = z
                hi, lo = _split_bf16(_neg_softplus(z))
                split_ring[slot, hd, 0:ATT_TILE, :] = hi
                split_ring[slot, hd, ATT_TILE:, :] = lo

            it = _Item(tab, i, 0, SB_STAGES)
            for hd in HEADS:
                z_ring[slot, hd] = _scores(it, k_ref, qh_ref, hd)

            for hd in HEADS:
                csum = _dot(tri_ref[...], split_ring[prev, hd])
                e_ring[slot, hd] = zm_ring[prev, hd] + csum
                sum_ring[slot, hd] = jnp.broadcast_to(csum[0:1, :], (SUBLANES, ATT_TILE))

            it = _Item(tab, i, 3, SB_STAGES)
            for hd in HEADS:
                tail = jnp.where(it.first == 1, state_tail_ref[it.query, hd], tail_ring[prev, hd])
                a = jnp.exp(_grouped(e_ring[prev, hd]) + tail[None]).reshape(ATT_TILE, ATT_TILE)
                acc = jnp.where(it.first == 1, state_acc_ref[it.query, hd], acc_ring[prev, hd])
                acc_ring[slot, hd] = acc + _dot(vt_ref[it.key, HEAD_ROWS[hd], :], a.astype(BF16))
                tail_ring[slot, hd] = tail + sum_ring[prev, hd]
            done, done_query = it.last, it.query

            def write_out():
                @pl.when(done == 1)
                def _():
                    for hd in HEADS:
                        state_tail_ref[done_query, hd] = tail_ring[slot, hd]
                        state_acc_ref[done_query, hd] = acc_ring[slot, hd]
                    out = jnp.concatenate([acc_ring[slot, hd] for hd in HEADS], axis=0)
                    o_ref[_tile_rows(done_query), :] = out.T.astype(BF16)

            return write_out

        return step

    _run_pipeline(pipeline_step(tab_ref), n_bodies)
    _run_pipeline(pipeline_step(far_tab_ref), _sb_far_items(state_tail_ref, far_tab_ref, n_blocks))


def _sb_attention(q, k, v, seq_len):
    n_blocks = seq_len // ATT_TILE
    n_tiles = SEQS_PER_STEP * n_blocks
    far_entries = (len(_causal_items(n_blocks, SB_NEAR_TILES, n_blocks))
                   + 2 * (SB_STAGES - 1) + RING)
    scratch = [pltpu.VMEM((ATT_TILE, 2 * ATT_TILE), BF16),
               pltpu.VMEM((2, ATT_TILE, ATT_TILE), F32),
               _tile_ring(F32),
               _tile_ring(F32),
               pltpu.VMEM((RING, HEADS_PER_STEP, 2 * ATT_TILE, ATT_TILE), BF16),
               _tile_ring(F32),
               _row_ring(),
               _row_ring(),
               _acc_ring(),
               pltpu.VMEM((n_tiles, HEADS_PER_STEP, SUBLANES, ATT_TILE), F32),
               pltpu.VMEM((n_tiles, HEADS_PER_STEP, HEAD_DIM, ATT_TILE), F32),
               pltpu.SMEM((ITEM_FIELDS, far_entries), jnp.int32)]
    return _attention_call(_sb_kernel, _causal_items(n_blocks, 0, SB_NEAR_TILES - 1), SB_STAGES,
                           seq_len, q, k, v, [], [], scratch, "stick_breaking_attention")


def _wo_conv_ffn_kernel(o_ref, o_halo_ref, h_ref, h_halo_ref, wo_ref, g_ref, wup_ref, cw_ref,
                        wdn_ref, gn_ref, out_ref, oext_ref, yext_ref, acc_ref,
                        *, tiles_per_seq, d_ff, final_norm):
    i = pl.program_id(0)
    seq_start = (i % tiles_per_seq) == 0
    tm = o_ref.shape[0]

    oext_ref[0:HALO, :] = o_halo_ref[...]
    oext_ref[HALO:, :] = o_ref[...]
    proj = _dot(oext_ref[...], wo_ref[...])
    h_halo = h_halo_ref[...] + proj[0:HALO]
    h_tile = h_ref[...] + proj[HALO:]
    y_halo = _rms_norm(h_halo, g_ref[...]).astype(BF16)
    yext_ref[0:HALO, :] = jnp.where(seq_start, jnp.zeros_like(y_halo), y_halo)
    yext_ref[HALO:, :] = _rms_norm(h_tile, g_ref[...]).astype(BF16)
    acc_ref[...] = h_tile

    def conv(u, taps):
        out = taps[CONV_WIDTH:CONV_WIDTH + 1, :]
        for kk in range(CONV_WIDTH):
            lo = HALO - (CONV_WIDTH - 1) + kk
            out = out + taps[kk:kk + 1, :] * u[lo:lo + tm, :]
        return out

    n_chunks = d_ff // FFN_CHUNK
    gate_cols = [slice(c * FFN_CHUNK, (c + 1) * FFN_CHUNK) for c in range(n_chunks)]
    val_cols = [slice(d_ff + c * FFN_CHUNK, d_ff + (c + 1) * FFN_CHUNK) for c in range(n_chunks)]
    up, act = {}, {}
    for step in range(n_chunks + 2):
        c = step
        if c < n_chunks:
            yext = yext_ref[...]
            up[c] = (_dot(yext, wup_ref[:, gate_cols[c]]), _dot(yext, wup_ref[:, val_cols[c]]))
        c = step - 2
        if 0 <= c < n_chunks:
            acc_ref[...] += _dot(act.pop(c), wdn_ref[gate_cols[c], :])
        c = step - 1
        if 0 <= c < n_chunks:
            u_gate, u_val = up.pop(c)
            gate = conv(u_gate, cw_ref[:, gate_cols[c]])
            val = conv(u_val, cw_ref[:, val_cols[c]])
            act[c] = (gate * (1.0 / (1.0 + jnp.exp(-gate))) * val).astype(BF16)
    h = acc_ref[...]
    out_ref[...] = _rms_norm(h, gn_ref[...]) if final_norm else h


def _wo_conv_ffn(o, h, wo, ffn_gain, wup, conv_taps, wdn, final_gain, final_norm, seq_len):
    m, d = h.shape
    d_ff = wdn.shape[0]
    row = pl.BlockSpec((ROW_TILE, d), lambda i: (i, 0))
    halo = pl.BlockSpec((HALO, d), lambda i: (jnp.maximum(i * (ROW_TILE // HALO) - 1, 0), 0))
    gain = pl.BlockSpec((1, d), lambda i: (0, 0))
    whole = lambda a: pl.BlockSpec(a.shape, lambda i: (0,) * a.ndim, pipeline_mode=pl.Buffered(1))
    return pl.pallas_call(
        functools.partial(_wo_conv_ffn_kernel, tiles_per_seq=seq_len // ROW_TILE,
                          d_ff=d_ff, final_norm=final_norm),
        grid=(m // ROW_TILE,),
        in_specs=[row, halo, row, halo, whole(wo), gain, whole(wup), whole(conv_taps), whole(wdn),
                  gain],
        out_specs=row,
        out_shape=jax.ShapeDtypeStruct((m, d), F32),
        scratch_shapes=[pltpu.VMEM((ROW_TILE + HALO, d), BF16),
                        pltpu.VMEM((ROW_TILE + HALO, d), BF16),
                        pltpu.VMEM((ROW_TILE, d), F32)],
        compiler_params=_params(1),
        name="wo_conv_ffn",
    )(o, o, h, h, wo, ffn_gain.reshape(1, d), wup, conv_taps, wdn, final_gain.reshape(1, d))


def kernel(x, attn_norm, w_qkv, w_o, rel_bias, ffn_norm, w_up, conv_w, conv_b, w_down, final_norm):
    b, s, d = x.shape
    depth = w_qkv.shape[0]
    h = x.reshape(b * s, d)
    bias_table = _bias_table(rel_bias)
    for i in range(depth):
        q, k, v = _norm_qkv(h, attn_norm[i], w_qkv[i].astype(BF16))
        if i % 2 == 0:
            o = _moba_attention(q, k, v, bias_table, s)
        else:
            o = _sb_attention(q, k, v, s)
        taps = jnp.concatenate(
            [conv_w[i], conv_b[i][None, :],
             jnp.zeros((8 - CONV_WIDTH - 1, conv_w.shape[-1]), F32)], axis=0)
        h = _wo_conv_ffn(o, h, w_o[i].astype(BF16), ffn_norm[i],
                         w_up[i].astype(BF16), taps, w_down[i].astype(BF16),
                         final_norm, i == depth - 1, s)
    return h.reshape(b, s, d)
```

```python
import functools

import numpy as np

import jax
import jax.numpy as jnp
from jax import lax
from jax.experimental import pallas as pl
from jax.experimental.pallas import tpu as pltpu

F32 = jnp.float32
BF16 = jnp.bfloat16

N_HEADS = 16
HEAD_DIM = 64
HEADS_PER_STEP = 2
LANES = HEADS_PER_STEP * HEAD_DIM
MOBA_BLOCK = 256
MOBA_TOPK = 3
ATT_TILE = MOBA_BLOCK
CONV_WIDTH = 3
REL_BUCKETS = 32
NORM_EPS = 1e-6
NEG = -1e30
ROW_TILE = 512
FFN_CHUNK = 256
HALO = 16
VMEM_LIMIT = 56 * 1024 * 1024
HEADS = range(HEADS_PER_STEP)
HEAD_ROWS = [slice(hd * HEAD_DIM, (hd + 1) * HEAD_DIM) for hd in HEADS]

REL_BUCKET_START = (0, 1, 2, 3, 4, 5, 6, 7, 8, 9, 10, 11, 12, 13, 14, 15, 16,
                    19, 21, 24, 27, 31, 35, 40, 46, 52, 59, 67, 77, 87, 99, 113)


def _dot(a, b):
    return jnp.dot(a, b, preferred_element_type=F32)


def _dot_nt(a, b):
    return lax.dot_general(a, b, (((1,), (1,)), ((), ())), preferred_element_type=F32)


def _rms_norm(x, g):
    ms = jnp.mean(x * x, axis=-1, keepdims=True)
    return x * lax.rsqrt(ms + NORM_EPS) * g


def _split_bf16(x):
    hi = x.astype(BF16)
    lo = (x - hi.astype(F32)).astype(BF16)
    return hi, lo


def _params(n_axes):
    return pltpu.CompilerParams(
        dimension_semantics=("arbitrary",) * n_axes,
        vmem_limit_bytes=VMEM_LIMIT)


def _norm_qkv_kernel(x_ref, g_ref, w_ref, q_ref, k_ref, v_ref, *, d_model, q_scale):
    y = _rms_norm(x_ref[...], g_ref[...]).astype(BF16)
    for idx, out in enumerate((q_ref, k_ref, v_ref)):
        r = _dot(y, w_ref[:, idx * d_model:(idx + 1) * d_model])
        if idx == 0:
            r = r * q_scale
        out[...] = r.astype(BF16)


def _norm_qkv(h, gain, w_bf16):
    m, d = h.shape
    row = pl.BlockSpec((ROW_TILE, d), lambda i: (i, 0))
    out = jax.ShapeDtypeStruct((m, d), BF16)
    return pl.pallas_call(
        functools.partial(_norm_qkv_kernel, d_model=d, q_scale=HEAD_DIM ** -0.5),
        grid=(m // ROW_TILE,),
        in_specs=[row,
                  pl.BlockSpec((1, d), lambda i: (0, 0)),
                  pl.BlockSpec((d, 3 * d), lambda i: (0, 0))],
        out_specs=[row, row, row],
        out_shape=[out, out, out],
        compiler_params=_params(1),
        name="norm_qkv",
    )(h, gain.reshape(1, d), w_bf16)


def _bias_table_kernel(rb_ref, out_ref):
    h = pl.program_id(0)
    key = lax.broadcasted_iota(jnp.int32, (ATT_TILE, ATT_TILE), 0)
    qry = lax.broadcasted_iota(jnp.int32, (ATT_TILE, ATT_TILE), 1)
    for delta in range(3):
        dist = jnp.maximum(qry - key + ATT_TILE * delta, 0)
        val = jnp.full((ATT_TILE, ATT_TILE), rb_ref[h, 0], F32)
        for b in range(1, REL_BUCKETS):
            val = jnp.where(dist >= REL_BUCKET_START[b], rb_ref[h, b], val)
        if delta == 0:
            val = val + jnp.where(key <= qry, 0.0, NEG)
        out_ref[0, delta] = val


def _bias_table(rel_bias):
    return pl.pallas_call(
        _bias_table_kernel,
        grid=(N_HEADS,),
        in_specs=[pl.BlockSpec(memory_space=pltpu.SMEM)],
        out_specs=pl.BlockSpec((1, 3, ATT_TILE, ATT_TILE), lambda h: (h, 0, 0, 0)),
        out_shape=jax.ShapeDtypeStruct((N_HEADS, 3, ATT_TILE, ATT_TILE), F32),
        compiler_params=_params(1),
        name="rel_bias_table",
    )(rel_bias)


ITEM_QUERY, ITEM_KEY, ITEM_FIRST, ITEM_LAST = range(4)
ITEM_FIELDS = 4
IDLE_ITEM = (0, 0, 1, 0)
RING = 4
SEQS_PER_STEP = 4
SUBLANES = 8


def _causal_items(n_blocks, nearest, farthest):
    items = []
    for tile in range(SEQS_PER_STEP * n_blocks):
        ts = range(nearest, min(tile % n_blocks, farthest) + 1)
        items += [(tile, tile - t, int(t == ts[0]), int(t == ts[-1])) for t in ts]
    return items


def _item_table(items, n_stages):
    steps = len(items) + n_stages - 1
    steps += -steps % RING
    lead = [IDLE_ITEM] * (n_stages - 1)
    trail = [IDLE_ITEM] * (steps + n_stages - 1 - len(lead) - len(items))
    return np.array(lead + list(items) + trail, np.int32).T, steps // RING


class _Item:
    def __init__(self, tab_ref, i, stage, n_stages):
        idx = i + (n_stages - 1) - stage
        self.query = tab_ref[ITEM_QUERY, idx]
        self.key = tab_ref[ITEM_KEY, idx]
        self.first = tab_ref[ITEM_FIRST, idx]
        self.last = tab_ref[ITEM_LAST, idx]


def _tile_rows(j):
    return pl.ds(pl.multiple_of(j * ATT_TILE, ATT_TILE), ATT_TILE)


def _grouped(x):
    return x.reshape(x.shape[0] // SUBLANES, SUBLANES, x.shape[1])


def _all_sublanes(x, op):
    for shift in (4, 2, 1):
        x = op(x, pltpu.roll(x, shift, 0))
    return x


def _store_head_queries(q_ref, qh_ref):
    q2 = q_ref[...]
    lane = lax.broadcasted_iota(jnp.int32, q2.shape, 1)
    for hd in HEADS:
        keep = (lane < HEAD_DIM) if hd == 0 else (lane >= HEAD_DIM)
        qh_ref[hd] = jnp.where(keep, q2, jnp.zeros_like(q2))


def _store_v_transposed(v_ref, vt_ref):
    for j in range(vt_ref.shape[0]):
        vj = v_ref[j * ATT_TILE:(j + 1) * ATT_TILE, :].astype(F32)
        vt_ref[j] = vj.T.astype(BF16)


def _scores(item, k_ref, qh_ref, hd):
    return _dot_nt(k_ref[_tile_rows(item.key), :], qh_ref[hd, _tile_rows(item.query), :])


def _first_grid_step():
    return (pl.program_id(0) == 0) & (pl.program_id(1) == 0)


def _run_pipeline(step, n_bodies):
    def body(m, carry):
        finish = [step(m * RING + slot, slot) for slot in range(RING)]
        for write_out in finish:
            write_out()
        return carry

    lax.fori_loop(0, n_bodies, body, 0)


def _attention_call(kernel, items, n_stages, seq_len, q, k, v, extra_inputs, extra_specs,
                    scratch, name):
    m, d = q.shape
    rows = SEQS_PER_STEP * seq_len
    n_blocks = seq_len // ATT_TILE
    table, n_bodies = _item_table(items, n_stages)
    seq_spec = pl.BlockSpec((rows, LANES), lambda hp, g: (g, hp))
    return pl.pallas_call(
        functools.partial(kernel, n_blocks=n_blocks, n_bodies=n_bodies),
        grid=(d // LANES, m // rows),
        in_specs=[pl.BlockSpec(memory_space=pltpu.SMEM), seq_spec, seq_spec, seq_spec] + extra_specs,
        out_specs=seq_spec,
        out_shape=jax.ShapeDtypeStruct((m, d), BF16),
        scratch_shapes=[pltpu.VMEM((HEADS_PER_STEP, rows, LANES), BF16),
                        pltpu.VMEM((SEQS_PER_STEP * n_blocks, LANES, ATT_TILE), BF16)]
        + scratch,
        compiler_params=_params(2),
        name=name,
    )(jnp.asarray(table), q, k, v, *extra_inputs)


def _tile_ring(dtype):
    return pltpu.VMEM((RING, HEADS_PER_STEP, ATT_TILE, ATT_TILE), dtype)


def _row_ring():
    return pltpu.VMEM((RING, HEADS_PER_STEP, SUBLANES, ATT_TILE), F32)


def _acc_ring():
    return pltpu.VMEM((RING, HEADS_PER_STEP, HEAD_DIM, ATT_TILE), F32)


def _zero(*refs):
    for ref in refs:
        ref[...] = jnp.zeros(ref.shape, ref.dtype)


MOBA_STAGES = 3


def _moba_block_choice(seq, k_ref, qh_ref, sel_ref, n_blocks):
    seq_len = n_blocks * ATT_TILE
    seq_rows = pl.ds(pl.multiple_of(seq * seq_len, seq_len), seq_len)
    means = []
    for j in range(n_blocks):
        kj = k_ref[_tile_rows(seq * n_blocks + j), :].astype(F32)
        means.append(jnp.sum(kj, axis=0, keepdims=True) * (1.0 / ATT_TILE))
    means.append(jnp.zeros((16 - n_blocks, LANES), F32))
    km_hi, km_lo = _split_bf16(jnp.concatenate(means, axis=0))

    blk = lax.broadcasted_iota(jnp.int32, (n_blocks, seq_len), 0)
    own = lax.shift_right_logical(lax.broadcasted_iota(jnp.int32, (n_blocks, seq_len), 1),
                                  ATT_TILE.bit_length() - 1)
    own_row = own[0:1, :]
    for hd in HEADS:
        qh = qh_ref[hd, seq_rows, :]
        gate = (_dot_nt(km_hi, qh) + _dot_nt(km_lo, qh))[0:n_blocks]
        gate = jnp.where(blk < own, gate, NEG)
        for j in range(n_blocks):
            gj = gate[j:j + 1, :]
            beats = (gate > gj) | ((gate == gj) & (blk < j))
            rank = jnp.sum(beats.astype(jnp.int32), axis=0, keepdims=True)
            attended = ((rank < MOBA_TOPK) & (own_row > j)) | (own_row == j)
            row = jnp.broadcast_to(jnp.where(attended, 0.0, NEG), (SUBLANES, seq_len))
            for n in range(n_blocks):
                sel_ref[hd, seq * n_blocks + j, n] = row[:, n * ATT_TILE:(n + 1) * ATT_TILE]


def _moba_kernel(tab_ref, q_ref, k_ref, v_ref, bias_ref, o_ref,
                 qh_ref, vt_ref, sel_ref, s_ring, p_ring, alpha_ring, max_ring,
                 denom_ring, done_denom_ring, acc_ring, *, n_blocks, n_bodies):
    @pl.when(_first_grid_step())
    def _():
        _zero(s_ring, p_ring, alpha_ring, max_ring, denom_ring, done_denom_ring, acc_ring)

    _store_head_queries(q_ref, qh_ref)
    _store_v_transposed(v_ref, vt_ref)

    for seq in range(SEQS_PER_STEP):
        _moba_block_choice(seq, k_ref, qh_ref, sel_ref, n_blocks)

    def step(i, slot):
        prev = (slot - 1) % RING

        it = _Item(tab_ref, i, 0, MOBA_STAGES)
        for hd in HEADS:
            s_ring[slot, hd] = _scores(it, k_ref, qh_ref, hd)

        it = _Item(tab_ref, i, 2, MOBA_STAGES)
        for hd in HEADS:
            acc = jnp.where(it.first == 1, 0.0, acc_ring[prev, hd])
            acc = (_grouped(acc) * alpha_ring[prev, hd][None]).reshape(acc.shape)
            acc_ring[slot, hd] = acc + _dot(vt_ref[it.key, HEAD_ROWS[hd], :], p_ring[prev, hd])
            done_denom_ring[slot, hd] = denom_ring[prev, hd]
        done, done_query = it.last, it.query

        it = _Item(tab_ref, i, 1, MOBA_STAGES)
        for hd in HEADS:
            logit = _grouped(s_ring[prev, hd] + bias_ref[hd, jnp.minimum(it.query - it.key, 2)])
            sel = sel_ref[hd, it.key, it.query % n_blocks]
            m_old = jnp.where(it.first == 1, NEG, max_ring[prev, hd])
            m_tile = _all_sublanes(jnp.max(logit, axis=0), jnp.maximum) + sel
            m_new = jnp.maximum(m_old, m_tile)
            alpha = jnp.exp(m_old - m_new)
            p = jnp.exp(logit - (m_new - sel)[None])
            denom_old = jnp.where(it.first == 1, 0.0, denom_ring[prev, hd])
            max_ring[slot, hd] = m_new
            denom_ring[slot, hd] = alpha * denom_old + _all_sublanes(jnp.sum(p, axis=0), jnp.add)
            alpha_ring[slot, hd] = alpha
            p_ring[slot, hd] = p.reshape(ATT_TILE, ATT_TILE).astype(BF16)

        def write_out():
            @pl.when(done == 1)
            def _():
                out = [(_grouped(acc_ring[slot, hd]) / done_denom_ring[slot, hd][None])
                       .reshape(HEAD_DIM, ATT_TILE) for hd in HEADS]
                o_ref[_tile_rows(done_query), :] = jnp.concatenate(out, axis=0).T.astype(BF16)

        return write_out

    _run_pipeline(step, n_bodies)


def _moba_attention(q, k, v, bias_table, seq_len):
    n_blocks = seq_len // ATT_TILE
    bias_spec = pl.BlockSpec((HEADS_PER_STEP, 3, ATT_TILE, ATT_TILE),
                             lambda hp, g: (hp, 0, 0, 0))
    scratch = [pltpu.VMEM((HEADS_PER_STEP, SEQS_PER_STEP * n_blocks, n_blocks, SUBLANES, ATT_TILE),
                          F32),
               _tile_ring(F32),
               _tile_ring(BF16),
               _row_ring(),
               _row_ring(),
               _row_ring(),
               _row_ring(),
               _acc_ring()]
    return _attention_call(_moba_kernel, _causal_items(n_blocks, 0, n_blocks), MOBA_STAGES,
                           seq_len, q, k, v, [bias_table], [bias_spec], scratch, "moba_attention")


SB_STAGES = 4
SB_NEAR_TILES = 2
SB_SKIP_BELOW = -110.0


def _neg_softplus(z):
    neg_z = -z
    return jnp.minimum(neg_z, 0.0) - jnp.log(1.0 + jnp.exp(jnp.minimum(z, neg_z)))


def _sb_far_items(state_tail_ref, far_tab_ref, n_blocks):
    def put(idx, item):
        for field in range(ITEM_FIELDS):
            far_tab_ref[field, idx] = jnp.int32(item[field])

    lead = SB_STAGES - 1
    for r in range(lead):
        put(r, IDLE_ITEM)
    count = jnp.int32(lead)
    for tile in range(SEQS_PER_STEP * n_blocks):
        n = tile % n_blocks
        if n < SB_NEAR_TILES:
            continue
        items = [(tile, tile - t, int(t == SB_NEAR_TILES), int(t == n))
                 for t in range(SB_NEAR_TILES, n + 1)]
        live = jnp.maximum(jnp.max(state_tail_ref[tile, 0]),
                           jnp.max(state_tail_ref[tile, 1])) >= SB_SKIP_BELOW

        @pl.when(live)
        def _(items=items, count=count):
            for r, item in enumerate(items):
                put(count + r, item)

        count = count + jnp.where(live, len(items), 0)
    for r in range(lead + RING):
        put(count + r, IDLE_ITEM)
    n_items = count - lead
    return jnp.where(n_items > 0, (n_items + lead + RING - 1) // RING, 0)


def _sb_kernel(tab_ref, q_ref, k_ref, v_ref, o_ref,
               qh_ref, vt_ref, tri_ref, mask_ref, z_ring, zm_ring, split_ring, e_ring,
               sum_ring, tail_ring, acc_ring, state_tail_ref, state_acc_ref, far_tab_ref,
               *, n_blocks, n_bodies):
    @pl.when(_first_grid_step())
    def _():
        key = lax.broadcasted_iota(jnp.int32, (ATT_TILE, ATT_TILE), 0)
        qry = lax.broadcasted_iota(jnp.int32, (ATT_TILE, ATT_TILE), 1)
        tri = (qry >= key).astype(BF16)
        tri_ref[...] = jnp.concatenate([tri, tri], axis=1)
        mask_ref[0] = jnp.zeros((ATT_TILE, ATT_TILE), F32)
        mask_ref[1] = jnp.where(key < qry, 0.0, NEG)
        _zero(z_ring, zm_ring, split_ring, e_ring, sum_ring, tail_ring, acc_ring)

    _store_head_queries(q_ref, qh_ref)
    _store_v_transposed(v_ref, vt_ref)
    _zero(state_tail_ref, state_acc_ref)

    def pipeline_step(tab):
        def step(i, slot):
            prev = (slot - 1) % RING

            it = _Item(tab, i, 1, SB_STAGES)
            diagonal = (it.query == it.key).astype(jnp.int32)
            for hd in HEADS:
                z = z_ring[prev, hd] + mask_ref[diagonal]
                zm_ring[slot, hd] = z
                hi, lo = _split_bf16(_neg_softplus(z))
                split_ring[slot, hd, 0:ATT_TILE, :] = hi
                split_ring[slot, hd, ATT_TILE:, :] = lo

            it = _Item(tab, i, 0, SB_STAGES)
            for hd in HEADS:
                z_ring[slot, hd] = _scores(it, k_ref, qh_ref, hd)

            for hd in HEADS:
                csum = _dot(tri_ref[...], split_ring[prev, hd])
                e_ring[slot, hd] = zm_ring[prev, hd] + csum
                sum_ring[slot, hd] = jnp.broadcast_to(csum[0:1, :], (SUBLANES, ATT_TILE))

            it = _Item(tab, i, 3, SB_STAGES)
            for hd in HEADS:
                tail = jnp.where(it.first == 1, state_tail_ref[it.query, hd], tail_ring[prev, hd])
                a = jnp.exp(_grouped(e_ring[prev, hd]) + tail[None]).reshape(ATT_TILE, ATT_TILE)
                acc = jnp.where(it.first == 1, state_acc_ref[it.query, hd], acc_ring[prev, hd])
                acc_ring[slot, hd] = acc + _dot(vt_ref[it.key, HEAD_ROWS[hd], :], a.astype(BF16))
                tail_ring[slot, hd] = tail + sum_ring[prev, hd]
            done, done_query = it.last, it.query

            def write_out():
                @pl.when(done == 1)
                def _():
                    for hd in HEADS:
                        state_tail_ref[done_query, hd] = tail_ring[slot, hd]
                        state_acc_ref[done_query, hd] = acc_ring[slot, hd]
                    out = jnp.concatenate([acc_ring[slot, hd] for hd in HEADS], axis=0)
                    o_ref[_tile_rows(done_query), :] = out.T.astype(BF16)

            return write_out

        return step

    _run_pipeline(pipeline_step(tab_ref), n_bodies)
    _run_pipeline(pipeline_step(far_tab_ref), _sb_far_items(state_tail_ref, far_tab_ref, n_blocks))


def _sb_attention(q, k, v, seq_len):
    n_blocks = seq_len // ATT_TILE
    n_tiles = SEQS_PER_STEP * n_blocks
    far_entries = (len(_causal_items(n_blocks, SB_NEAR_TILES, n_blocks))
                   + 2 * (SB_STAGES - 1) + RING)
    scratch = [pltpu.VMEM((ATT_TILE, 2 * ATT_TILE), BF16),
               pltpu.VMEM((2, ATT_TILE, ATT_TILE), F32),
               _tile_ring(F32),
               _tile_ring(F32),
               pltpu.VMEM((RING, HEADS_PER_STEP, 2 * ATT_TILE, ATT_TILE), BF16),
               _tile_ring(F32),
               _row_ring(),
               _row_ring(),
               _acc_ring(),
               pltpu.VMEM((n_tiles, HEADS_PER_STEP, SUBLANES, ATT_TILE), F32),
               pltpu.VMEM((n_tiles, HEADS_PER_STEP, HEAD_DIM, ATT_TILE), F32),
               pltpu.SMEM((ITEM_FIELDS, far_entries), jnp.int32)]
    return _attention_call(_sb_kernel, _causal_items(n_blocks, 0, SB_NEAR_TILES - 1), SB_STAGES,
                           seq_len, q, k, v, [], [], scratch, "stick_breaking_attention")


def _wo_conv_ffn_kernel(o_ref, o_halo_ref, h_ref, h_halo_ref, wo_ref, g_ref, wup_ref, cw_ref,
                        wdn_ref, gn_ref, out_ref, oext_ref, yext_ref, acc_ref,
                        *, tiles_per_seq, d_ff, final_norm):
    i = pl.program_id(0)
    seq_start = (i % tiles_per_seq) == 0
    tm = o_ref.shape[0]

    oext_ref[0:HALO, :] = o_halo_ref[...]
    oext_ref[HALO:, :] = o_ref[...]
    proj = _dot(oext_ref[...], wo_ref[...])
    h_halo = h_halo_ref[...] + proj[0:HALO]
    h_tile = h_ref[...] + proj[HALO:]
    y_halo = _rms_norm(h_halo, g_ref[...]).astype(BF16)
    yext_ref[0:HALO, :] = jnp.where(seq_start, jnp.zeros_like(y_halo), y_halo)
    yext_ref[HALO:, :] = _rms_norm(h_tile, g_ref[...]).astype(BF16)
    acc_ref[...] = h_tile

    def conv(u, taps):
        out = taps[CONV_WIDTH:CONV_WIDTH + 1, :]
        for kk in range(CONV_WIDTH):
            lo = HALO - (CONV_WIDTH - 1) + kk
            out = out + taps[kk:kk + 1, :] * u[lo:lo + tm, :]
        return out

    n_chunks = d_ff // FFN_CHUNK
    gate_cols = [slice(c * FFN_CHUNK, (c + 1) * FFN_CHUNK) for c in range(n_chunks)]
    val_cols = [slice(d_ff + c * FFN_CHUNK, d_ff + (c + 1) * FFN_CHUNK) for c in range(n_chunks)]
    up, act = {}, {}
    for step in range(n_chunks + 2):
        c = step
        if c < n_chunks:
            yext = yext_ref[...]
            up[c] = (_dot(yext, wup_ref[:, gate_cols[c]]), _dot(yext, wup_ref[:, val_cols[c]]))
        c = step - 2
        if 0 <= c < n_chunks:
            acc_ref[...] += _dot(act.pop(c), wdn_ref[gate_cols[c], :])
        c = step - 1
        if 0 <= c < n_chunks:
            u_gate, u_val = up.pop(c)
            gate = conv(u_gate, cw_ref[:, gate_cols[c]])
            val = conv(u_val, cw_ref[:, val_cols[c]])
            act[c] = (gate * (1.0 / (1.0 + jnp.exp(-gate))) * val).astype(BF16)
    h = acc_ref[...]
    out_ref[...] = _rms_norm(h, gn_ref[...]) if final_norm else h


def _wo_conv_ffn(o, h, wo, ffn_gain, wup, conv_taps, wdn, final_gain, final_norm, seq_len):
    m, d = h.shape
    d_ff = wdn.shape[0]
    row = pl.BlockSpec((ROW_TILE, d), lambda i: (i, 0))
    halo = pl.BlockSpec((HALO, d), lambda i: (jnp.maximum(i * (ROW_TILE // HALO) - 1, 0), 0))
    gain = pl.BlockSpec((1, d), lambda i: (0, 0))
    whole = lambda a: pl.BlockSpec(a.shape, lambda i: (0,) * a.ndim, pipeline_mode=pl.Buffered(1))
    return pl.pallas_call(
        functools.partial(_wo_conv_ffn_kernel, tiles_per_seq=seq_len // ROW_TILE,
                          d_ff=d_ff, final_norm=final_norm),
        grid=(m // ROW_TILE,),
        in_specs=[row, halo, row, halo, whole(wo), gain, whole(wup), whole(conv_taps), whole(wdn),
                  gain],
        out_specs=row,
        out_shape=jax.ShapeDtypeStruct((m, d), F32),
        scratch_shapes=[pltpu.VMEM((ROW_TILE + HALO, d), BF16),
                        pltpu.VMEM((ROW_TILE + HALO, d), BF16),
                        pltpu.VMEM((ROW_TILE, d), F32)],
        compiler_params=_params(1),
        name="wo_conv_ffn",
    )(o, o, h, h, wo, ffn_gain.reshape(1, d), wup, conv_taps, wdn, final_gain.reshape(1, d))


def kernel(x, attn_norm, w_qkv, w_o, rel_bias, ffn_norm, w_up, conv_w, conv_b, w_down, final_norm):
    b, s, d = x.shape
    depth = w_qkv.shape[0]
    h = x.reshape(b * s, d)
    bias_table = _bias_table(rel_bias)
    for i in range(depth):
        q, k, v = _norm_qkv(h, attn_norm[i], w_qkv[i].astype(BF16))
        if i % 2 == 0:
            o = _moba_attention(q, k, v, bias_table, s)
        else:
            o = _sb_attention(q, k, v, s)
        taps = jnp.concatenate(
            [conv_w[i], conv_b[i][None, :],
             jnp.zeros((8 - CONV_WIDTH - 1, conv_w.shape[-1]), F32)], axis=0)
        h = _wo_conv_ffn(o, h, w_o[i].astype(BF16), ffn_norm[i],
                         w_up[i].astype(BF16), taps, w_down[i].astype(BF16),
                         final_norm, i == depth - 1, s)
    return h.reshape(b, s, d)
```

```python
import functools

import numpy as np

import jax
import jax.numpy as jnp
from jax import lax
from jax.experimental import pallas as pl
from jax.experimental.pallas import tpu as pltpu

F32 = jnp.float32
BF16 = jnp.bfloat16

N_HEADS = 16
HEAD_DIM = 64
HEADS_PER_STEP = 2
LANES = HEADS_PER_STEP * HEAD_DIM
MOBA_BLOCK = 256
MOBA_TOPK = 3
ATT_TILE = MOBA_BLOCK
CONV_WIDTH = 3
REL_BUCKETS = 32
NORM_EPS = 1e-6
NEG = -1e30
ROW_TILE = 512
FFN_CHUNK = 256
HALO = 16
VMEM_LIMIT = 56 * 1024 * 1024
HEADS = range(HEADS_PER_STEP)
HEAD_ROWS = [slice(hd * HEAD_DIM, (hd + 1) * HEAD_DIM) for hd in HEADS]

REL_BUCKET_START = (0, 1, 2, 3, 4, 5, 6, 7, 8, 9, 10, 11, 12, 13, 14, 15, 16,
                    19, 21, 24, 27, 31, 35, 40, 46, 52, 59, 67, 77, 87, 99, 113)


def _dot(a, b):
    return jnp.dot(a, b, preferred_element_type=F32)


def _dot_nt(a, b):
    return lax.dot_general(a, b, (((1,), (1,)), ((), ())), preferred_element_type=F32)


def _rms_norm(x, g):
    ms = jnp.mean(x * x, axis=-1, keepdims=True)
    return x * lax.rsqrt(ms + NORM_EPS) * g


def _split_bf16(x):
    hi = x.astype(BF16)
    lo = (x - hi.astype(F32)).astype(BF16)
    return hi, lo


def _params(n_axes):
    return pltpu.CompilerParams(
        dimension_semantics=("arbitrary",) * n_axes,
        vmem_limit_bytes=VMEM_LIMIT)


def _norm_qkv_kernel(x_ref, g_ref, w_ref, q_ref, k_ref, v_ref, *, d_model, q_scale):
    y = _rms_norm(x_ref[...], g_ref[...]).astype(BF16)
    for idx, out in enumerate((q_ref, k_ref, v_ref)):
        r = _dot(y, w_ref[:, idx * d_model:(idx + 1) * d_model])
        if idx == 0:
            r = r * q_scale
        out[...] = r.astype(BF16)


def _norm_qkv(h, gain, w_bf16):
    m, d = h.shape
    row = pl.BlockSpec((ROW_TILE, d), lambda i: (i, 0))
    out = jax.ShapeDtypeStruct((m, d), BF16)
    return pl.pallas_call(
        functools.partial(_norm_qkv_kernel, d_model=d, q_scale=HEAD_DIM ** -0.5),
        grid=(m // ROW_TILE,),
        in_specs=[row,
                  pl.BlockSpec((1, d), lambda i: (0, 0)),
                  pl.BlockSpec((d, 3 * d), lambda i: (0, 0))],
        out_specs=[row, row, row],
        out_shape=[out, out, out],
        compiler_params=_params(1),
        name="norm_qkv",
    )(h, gain.reshape(1, d), w_bf16)


def _bias_table_kernel(rb_ref, out_ref):
    h = pl.program_id(0)
    key = lax.broadcasted_iota(jnp.int32, (ATT_TILE, ATT_TILE), 0)
    qry = lax.broadcasted_iota(jnp.int32, (ATT_TILE, ATT_TILE), 1)
    for delta in range(3):
        dist = jnp.maximum(qry - key + ATT_TILE * delta, 0)
        val = jnp.full((ATT_TILE, ATT_TILE), rb_ref[h, 0], F32)
        for b in range(1, REL_BUCKETS):
            val = jnp.where(dist >= REL_BUCKET_START[b], rb_ref[h, b], val)
        if delta == 0:
            val = val + jnp.where(key <= qry, 0.0, NEG)
        out_ref[0, delta] = val


def _bias_table(rel_bias):
    return pl.pallas_call(
        _bias_table_kernel,
        grid=(N_HEADS,),
        in_specs=[pl.BlockSpec(memory_space=pltpu.SMEM)],
        out_specs=pl.BlockSpec((1, 3, ATT_TILE, ATT_TILE), lambda h: (h, 0, 0, 0)),
        out_shape=jax.ShapeDtypeStruct((N_HEADS, 3, ATT_TILE, ATT_TILE), F32),
        compiler_params=_params(1),
        name="rel_bias_table",
    )(rel_bias)


ITEM_QUERY, ITEM_KEY, ITEM_FIRST, ITEM_LAST = range(4)
ITEM_FIELDS = 4
IDLE_ITEM = (0, 0, 1, 0)
RING = 8
SEQS_PER_STEP = 4
SUBLANES = 8


def _causal_items(n_blocks, nearest, farthest):
    items = []
    for tile in range(SEQS_PER_STEP * n_blocks):
        ts = range(nearest, min(tile % n_blocks, farthest) + 1)
        items += [(tile, tile - t, int(t == ts[0]), int(t == ts[-1])) for t in ts]
    return items


def _item_table(items, n_stages):
    steps = len(items) + n_stages - 1
    steps += -steps % RING
    lead = [IDLE_ITEM] * (n_stages - 1)
    trail = [IDLE_ITEM] * (steps + n_stages - 1 - len(lead) - len(items))
    return np.array(lead + list(items) + trail, np.int32).T, steps // RING


class _Item:
    def __init__(self, tab_ref, i, stage, n_stages):
        idx = i + (n_stages - 1) - stage
        self.query = tab_ref[ITEM_QUERY, idx]
        self.key = tab_ref[ITEM_KEY, idx]
        self.first = tab_ref[ITEM_FIRST, idx]
        self.last = tab_ref[ITEM_LAST, idx]


def _tile_rows(j):
    return pl.ds(pl.multiple_of(j * ATT_TILE, ATT_TILE), ATT_TILE)


def _grouped(x):
    return x.reshape(x.shape[0] // SUBLANES, SUBLANES, x.shape[1])


def _all_sublanes(x, op):
    for shift in (4, 2, 1):
        x = op(x, pltpu.roll(x, shift, 0))
    return x


def _store_head_queries(q_ref, qh_ref):
    q2 = q_ref[...]
    lane = lax.broadcasted_iota(jnp.int32, q2.shape, 1)
    for hd in HEADS:
        keep = (lane < HEAD_DIM) if hd == 0 else (lane >= HEAD_DIM)
        qh_ref[hd] = jnp.where(keep, q2, jnp.zeros_like(q2))


def _store_v_transposed(v_ref, vt_ref):
    for j in range(vt_ref.shape[0]):
        vj = v_ref[j * ATT_TILE:(j + 1) * ATT_TILE, :].astype(F32)
        vt_ref[j] = vj.T.astype(BF16)


def _scores(item, k_ref, qh_ref, hd):
    return _dot_nt(k_ref[_tile_rows(item.key), :], qh_ref[hd, _tile_rows(item.query), :])


def _first_grid_step():
    return (pl.program_id(0) == 0) & (pl.program_id(1) == 0)


def _run_pipeline(step, n_bodies):
    def body(m, carry):
        finish = [step(m * RING + slot, slot) for slot in range(RING)]
        for write_out in finish:
            write_out()
        return carry

    lax.fori_loop(0, n_bodies, body, 0)


def _attention_call(kernel, items, n_stages, seq_len, q, k, v, extra_inputs, extra_specs,
                    scratch, name):
    m, d = q.shape
    rows = SEQS_PER_STEP * seq_len
    n_blocks = seq_len // ATT_TILE
    table, n_bodies = _item_table(items, n_stages)
    seq_spec = pl.BlockSpec((rows, LANES), lambda hp, g: (g, hp))
    return pl.pallas_call(
        functools.partial(kernel, n_blocks=n_blocks, n_bodies=n_bodies),
        grid=(d // LANES, m // rows),
        in_specs=[pl.BlockSpec(memory_space=pltpu.SMEM), seq_spec, seq_spec, seq_spec] + extra_specs,
        out_specs=seq_spec,
        out_shape=jax.ShapeDtypeStruct((m, d), BF16),
        scratch_shapes=[pltpu.VMEM((HEADS_PER_STEP, rows, LANES), BF16),
                        pltpu.VMEM((SEQS_PER_STEP * n_blocks, LANES, ATT_TILE), BF16)]
        + scratch,
        compiler_params=_params(2),
        name=name,
    )(jnp.asarray(table), q, k, v, *extra_inputs)


def _tile_ring(dtype):
    return pltpu.VMEM((RING, HEADS_PER_STEP, ATT_TILE, ATT_TILE), dtype)


def _row_ring():
    return pltpu.VMEM((RING, HEADS_PER_STEP, SUBLANES, ATT_TILE), F32)


def _acc_ring():
    return pltpu.VMEM((RING, HEADS_PER_STEP, HEAD_DIM, ATT_TILE), F32)


def _zero(*refs):
    for ref in refs:
        ref[...] = jnp.zeros(ref.shape, ref.dtype)


MOBA_STAGES = 3


def _moba_block_choice(seq, k_ref, qh_ref, sel_ref, n_blocks):
    seq_len = n_blocks * ATT_TILE
    seq_rows = pl.ds(pl.multiple_of(seq * seq_len, seq_len), seq_len)
    means = []
    for j in range(n_blocks):
        kj = k_ref[_tile_rows(seq * n_blocks + j), :].astype(F32)
        means.append(jnp.sum(kj, axis=0, keepdims=True) * (1.0 / ATT_TILE))
    means.append(jnp.zeros((16 - n_blocks, LANES), F32))
    km_hi, km_lo = _split_bf16(jnp.concatenate(means, axis=0))

    blk = lax.broadcasted_iota(jnp.int32, (n_blocks, seq_len), 0)
    own = lax.shift_right_logical(lax.broadcasted_iota(jnp.int32, (n_blocks, seq_len), 1),
                                  ATT_TILE.bit_length() - 1)
    own_row = own[0:1, :]
    for hd in HEADS:
        qh = qh_ref[hd, seq_rows, :]
        gate = (_dot_nt(km_hi, qh) + _dot_nt(km_lo, qh))[0:n_blocks]
        gate = jnp.where(blk < own, gate, NEG)
        for j in range(n_blocks):
            gj = gate[j:j + 1, :]
            beats = (gate > gj) | ((gate == gj) & (blk < j))
            rank = jnp.sum(beats.astype(jnp.int32), axis=0, keepdims=True)
            attended = ((rank < MOBA_TOPK) & (own_row > j)) | (own_row == j)
            row = jnp.broadcast_to(jnp.where(attended, 0.0, NEG), (SUBLANES, seq_len))
            for n in range(n_blocks):
                sel_ref[hd, seq * n_blocks + j, n] = row[:, n * ATT_TILE:(n + 1) * ATT_TILE]


def _moba_kernel(tab_ref, q_ref, k_ref, v_ref, bias_ref, o_ref,
                 qh_ref, vt_ref, sel_ref, s_ring, p_ring, alpha_ring, max_ring,
                 denom_ring, done_denom_ring, acc_ring, *, n_blocks, n_bodies):
    @pl.when(_first_grid_step())
    def _():
        _zero(s_ring, p_ring, alpha_ring, max_ring, denom_ring, done_denom_ring, acc_ring)

    _store_head_queries(q_ref, qh_ref)
    _store_v_transposed(v_ref, vt_ref)

    for seq in range(SEQS_PER_STEP):
        _moba_block_choice(seq, k_ref, qh_ref, sel_ref, n_blocks)

    def step(i, slot):
        prev = (slot - 1) % RING

        it = _Item(tab_ref, i, 0, MOBA_STAGES)
        for hd in HEADS:
            s_ring[slot, hd] = _scores(it, k_ref, qh_ref, hd)

        it = _Item(tab_ref, i, 2, MOBA_STAGES)
        for hd in HEADS:
            acc = jnp.where(it.first == 1, 0.0, acc_ring[prev, hd])
            acc = (_grouped(acc) * alpha_ring[prev, hd][None]).reshape(acc.shape)
            acc_ring[slot, hd] = acc + _dot(vt_ref[it.key, HEAD_ROWS[hd], :], p_ring[prev, hd])
            done_denom_ring[slot, hd] = denom_ring[prev, hd]
        done, done_query = it.last, it.query

        it = _Item(tab_ref, i, 1, MOBA_STAGES)
        for hd in HEADS:
            logit = _grouped(s_ring[prev, hd] + bias_ref[hd, jnp.minimum(it.query - it.key, 2)])
            sel = sel_ref[hd, it.key, it.query % n_blocks]
            m_old = jnp.where(it.first == 1, NEG, max_ring[prev, hd])
            m_tile = _all_sublanes(jnp.max(logit, axis=0), jnp.maximum) + sel
            m_new = jnp.maximum(m_old, m_tile)
            alpha = jnp.exp(m_old - m_new)
            p = jnp.exp(logit - (m_new - sel)[None])
            denom_old = jnp.where(it.first == 1, 0.0, denom_ring[prev, hd])
            max_ring[slot, hd] = m_new
            denom_ring[slot, hd] = alpha * denom_old + _all_sublanes(jnp.sum(p, axis=0), jnp.add)
            alpha_ring[slot, hd] = alpha
            p_ring[slot, hd] = p.reshape(ATT_TILE, ATT_TILE).astype(BF16)

        def write_out():
            @pl.when(done == 1)
            def _():
                out = [(_grouped(acc_ring[slot, hd]) / done_denom_ring[slot, hd][None])
                       .reshape(HEAD_DIM, ATT_TILE) for hd in HEADS]
                o_ref[_tile_rows(done_query), :] = jnp.concatenate(out, axis=0).T.astype(BF16)

        return write_out

    _run_pipeline(step, n_bodies)


def _moba_attention(q, k, v, bias_table, seq_len):
    n_blocks = seq_len // ATT_TILE
    bias_spec = pl.BlockSpec((HEADS_PER_STEP, 3, ATT_TILE, ATT_TILE),
                             lambda hp, g: (hp, 0, 0, 0))
    scratch = [pltpu.VMEM((HEADS_PER_STEP, SEQS_PER_STEP * n_blocks, n_blocks, SUBLANES, ATT_TILE),
                          F32),
               _tile_ring(F32),
               _tile_ring(BF16),
               _row_ring(),
               _row_ring(),
               _row_ring(),
               _row_ring(),
               _acc_ring()]
    return _attention_call(_moba_kernel, _causal_items(n_blocks, 0, n_blocks), MOBA_STAGES,
                           seq_len, q, k, v, [bias_table], [bias_spec], scratch, "moba_attention")


SB_STAGES = 4
SB_NEAR_TILES = 2
SB_SKIP_BELOW = -110.0


def _neg_softplus(z):
    neg_z = -z
    return jnp.minimum(neg_z, 0.0) - jnp.log(1.0 + jnp.exp(jnp.minimum(z, neg_z)))


def _sb_far_items(state_tail_ref, far_tab_ref, n_blocks):
    def put(idx, item):
        for field in range(ITEM_FIELDS):
            far_tab_ref[field, idx] = jnp.int32(item[field])

    lead = SB_STAGES - 1
    for r in range(lead):
        put(r, IDLE_ITEM)
    count = jnp.int32(lead)
    for tile in range(SEQS_PER_STEP * n_blocks):
        n = tile % n_blocks
        if n < SB_NEAR_TILES:
            continue
        items = [(tile, tile - t, int(t == SB_NEAR_TILES), int(t == n))
                 for t in range(SB_NEAR_TILES, n + 1)]
        live = jnp.maximum(jnp.max(state_tail_ref[tile, 0]),
                           jnp.max(state_tail_ref[tile, 1])) >= SB_SKIP_BELOW

        @pl.when(live)
        def _(items=items, count=count):
            for r, item in enumerate(items):
                put(count + r, item)

        count = count + jnp.where(live, len(items), 0)
    for r in range(lead + RING):
        put(count + r, IDLE_ITEM)
    n_items = count - lead
    return jnp.where(n_items > 0, (n_items + lead + RING - 1) // RING, 0)


def _sb_kernel(tab_ref, q_ref, k_ref, v_ref, o_ref,
               qh_ref, vt_ref, tri_ref, mask_ref, z_ring, zm_ring, split_ring, e_ring,
               sum_ring, tail_ring, acc_ring, state_tail_ref, state_acc_ref, far_tab_ref,
               *, n_blocks, n_bodies):
    @pl.when(_first_grid_step())
    def _():
        key = lax.broadcasted_iota(jnp.int32, (ATT_TILE, ATT_TILE), 0)
        qry = lax.broadcasted_iota(jnp.int32, (ATT_TILE, ATT_TILE), 1)
        tri = (qry >= key).astype(BF16)
        tri_ref[...] = jnp.concatenate([tri, tri], axis=1)
        mask_ref[0] = jnp.zeros((ATT_TILE, ATT_TILE), F32)
        mask_ref[1] = jnp.where(key < qry, 0.0, NEG)
        _zero(z_ring, zm_ring, split_ring, e_ring, sum_ring, tail_ring, acc_ring)

    _store_head_queries(q_ref, qh_ref)
    _store_v_transposed(v_ref, vt_ref)
    _zero(state_tail_ref, state_acc_ref)

    def pipeline_step(tab):
        def step(i, slot):
            prev = (slot - 1) % RING

            it = _Item(tab, i, 1, SB_STAGES)
            diagonal = (it.query == it.key).astype(jnp.int32)
            for hd in HEADS:
                z = z_ring[prev, hd] + mask_ref[diagonal]
                zm_ring[slot, hd] = z
                hi, lo = _split_bf16(_neg_softplus(z))
                split_ring[slot, hd, 0:ATT_TILE, :] = hi
                split_ring[slot, hd, ATT_TILE:, :] = lo

            it = _Item(tab, i, 0, SB_STAGES)
            for hd in HEADS:
                z_ring[slot, hd] = _scores(it, k_ref, qh_ref, hd)

            for hd in HEADS:
                csum = _dot(tri_ref[...], split_ring[prev, hd])
                e_ring[slot, hd] = zm_ring[prev, hd] + csum
                sum_ring[slot, hd] = jnp.broadcast_to(csum[0:1, :], (SUBLANES, ATT_TILE))

            it = _Item(tab, i, 3, SB_STAGES)
            for hd in HEADS:
                tail = jnp.where(it.first == 1, state_tail_ref[it.query, hd], tail_ring[prev, hd])
                a = jnp.exp(_grouped(e_ring[prev, hd]) + tail[None]).reshape(ATT_TILE, ATT_TILE)
                acc = jnp.where(it.first == 1, state_acc_ref[it.query, hd], acc_ring[prev, hd])
                acc_ring[slot, hd] = acc + _dot(vt_ref[it.key, HEAD_ROWS[hd], :], a.astype(BF16))
                tail_ring[slot, hd] = tail + sum_ring[prev, hd]
            done, done_query = it.last, it.query

            def write_out():
                @pl.when(done == 1)
                def _():
                    for hd in HEADS:
                        state_tail_ref[done_query, hd] = tail_ring[slot, hd]
                        state_acc_ref[done_query, hd] = acc_ring[slot, hd]
                    out = jnp.concatenate([acc_ring[slot, hd] for hd in HEADS], axis=0)
                    o_ref[_tile_rows(done_query), :] = out.T.astype(BF16)

            return write_out

        return step

    _run_pipeline(pipeline_step(tab_ref), n_bodies)
    _run_pipeline(pipeline_step(far_tab_ref), _sb_far_items(state_tail_ref, far_tab_ref, n_blocks))


def _sb_attention(q, k, v, seq_len):
    n_blocks = seq_len // ATT_TILE
    n_tiles = SEQS_PER_STEP * n_blocks
    far_entries = (len(_causal_items(n_blocks, SB_NEAR_TILES, n_blocks))
                   + 2 * (SB_STAGES - 1) + RING)
    scratch = [pltpu.VMEM((ATT_TILE, 2 * ATT_TILE), BF16),
               pltpu.VMEM((2, ATT_TILE, ATT_TILE), F32),
               _tile_ring(F32),
               _tile_ring(F32),
               pltpu.VMEM((RING, HEADS_PER_STEP, 2 * ATT_TILE, ATT_TILE), BF16),
               _tile_ring(F32),
               _row_ring(),
               _row_ring(),
               _acc_ring(),
               pltpu.VMEM((n_tiles, HEADS_PER_STEP, SUBLANES, ATT_TILE), F32),
               pltpu.VMEM((n_tiles, HEADS_PER_STEP, HEAD_DIM, ATT_TILE), F32),
               pltpu.SMEM((ITEM_FIELDS, far_entries), jnp.int32)]
    return _attention_call(_sb_kernel, _causal_items(n_blocks, 0, SB_NEAR_TILES - 1), SB_STAGES,
                           seq_len, q, k, v, [], [], scratch, "stick_breaking_attention")


def _wo_conv_ffn_kernel(o_ref, o_halo_ref, h_ref, h_halo_ref, wo_ref, g_ref, wup_ref, cw_ref,
                        wdn_ref, gn_ref, out_ref, oext_ref, yext_ref, acc_ref,
                        *, tiles_per_seq, d_ff, final_norm):
    i = pl.program_id(0)
    seq_start = (i % tiles_per_seq) == 0
    tm = o_ref.shape[0]

    oext_ref[0:HALO, :] = o_halo_ref[...]
    oext_ref[HALO:, :] = o_ref[...]
    proj = _dot(oext_ref[...], wo_ref[...])
    h_halo = h_halo_ref[...] + proj[0:HALO]
    h_tile = h_ref[...] + proj[HALO:]
    y_halo = _rms_norm(h_halo, g_ref[...]).astype(BF16)
    yext_ref[0:HALO, :] = jnp.where(seq_start, jnp.zeros_like(y_halo), y_halo)
    yext_ref[HALO:, :] = _rms_norm(h_tile, g_ref[...]).astype(BF16)
    acc_ref[...] = h_tile

    def conv(u, taps):
        out = taps[CONV_WIDTH:CONV_WIDTH + 1, :]
        for kk in range(CONV_WIDTH):
            lo = HALO - (CONV_WIDTH - 1) + kk
            out = out + taps[kk:kk + 1, :] * u[lo:lo + tm, :]
        return out

    n_chunks = d_ff // FFN_CHUNK
    gate_cols = [slice(c * FFN_CHUNK, (c + 1) * FFN_CHUNK) for c in range(n_chunks)]
    val_cols = [slice(d_ff + c * FFN_CHUNK, d_ff + (c + 1) * FFN_CHUNK) for c in range(n_chunks)]
    up, act = {}, {}
    for step in range(n_chunks + 2):
        c = step
        if c < n_chunks:
            yext = yext_ref[...]
            up[c] = (_dot(yext, wup_ref[:, gate_cols[c]]), _dot(yext, wup_ref[:, val_cols[c]]))
        c = step - 2
        if 0 <= c < n_chunks:
            acc_ref[...] += _dot(act.pop(c), wdn_ref[gate_cols[c], :])
        c = step - 1
        if 0 <= c < n_chunks:
            u_gate, u_val = up.pop(c)
            gate = conv(u_gate, cw_ref[:, gate_cols[c]])
            val = conv(u_val, cw_ref[:, val_cols[c]])
            act[c] = (gate * (1.0 / (1.0 + jnp.exp(-gate))) * val).astype(BF16)
    h = acc_ref[...]
    out_ref[...] = _rms_norm(h, gn_ref[...]) if final_norm else h


def _wo_conv_ffn(o, h, wo, ffn_gain, wup, conv_taps, wdn, final_gain, final_norm, seq_len):
    m, d = h.shape
    d_ff = wdn.shape[0]
    row = pl.BlockSpec((ROW_TILE, d), lambda i: (i, 0))
    halo = pl.BlockSpec((HALO, d), lambda i: (jnp.maximum(i * (ROW_TILE // HALO) - 1, 0), 0))
    gain = pl.BlockSpec((1, d), lambda i: (0, 0))
    whole = lambda a: pl.BlockSpec(a.shape, lambda i: (0,) * a.ndim, pipeline_mode=pl.Buffered(1))
    return pl.pallas_call(
        functools.partial(_wo_conv_ffn_kernel, tiles_per_seq=seq_len // ROW_TILE,
                          d_ff=d_ff, final_norm=final_norm),
        grid=(m // ROW_TILE,),
        in_specs=[row, halo, row, halo, whole(wo), gain, whole(wup), whole(conv_taps), whole(wdn),
                  gain],
        out_specs=row,
        out_shape=jax.ShapeDtypeStruct((m, d), F32),
        scratch_shapes=[pltpu.VMEM((ROW_TILE + HALO, d), BF16),
                        pltpu.VMEM((ROW_TILE + HALO, d), BF16),
                        pltpu.VMEM((ROW_TILE, d), F32)],
        compiler_params=_params(1),
        name="wo_conv_ffn",
    )(o, o, h, h, wo, ffn_gain.reshape(1, d), wup, conv_taps, wdn, final_gain.reshape(1, d))


def kernel(x, attn_norm, w_qkv, w_o, rel_bias, ffn_norm, w_up, conv_w, conv_b, w_down, final_norm):
    b, s, d = x.shape
    depth = w_qkv.shape[0]
    h = x.reshape(b * s, d)
    bias_table = _bias_table(rel_bias)
    for i in range(depth):
        q, k, v = _norm_qkv(h, attn_norm[i], w_qkv[i].astype(BF16))
        if i % 2 == 0:
            o = _moba_attention(q, k, v, bias_table, s)
        else:
            o = _sb_attention(q, k, v, s)
        taps = jnp.concatenate(
            [conv_w[i], conv_b[i][None, :],
             jnp.zeros((8 - CONV_WIDTH - 1, conv_w.shape[-1]), F32)], axis=0)
        h = _wo_conv_ffn(o, h, w_o[i].astype(BF16), ffn_norm[i],
                         w_up[i].astype(BF16), taps, w_down[i].astype(BF16),
                         final_norm, i == depth - 1, s)
    return h.reshape(b, s, d)
```

```python
import functools

import numpy as np

import jax
import jax.numpy as jnp
from jax import lax
from jax.experimental import pallas as pl
from jax.experimental.pallas import tpu as pltpu

F32 = jnp.float32
BF16 = jnp.bfloat16

N_HEADS = 16
HEAD_DIM = 64
HEADS_PER_STEP = 2
LANES = HEADS_PER_STEP * HEAD_DIM
MOBA_BLOCK = 256
MOBA_TOPK = 3
ATT_TILE = MOBA_BLOCK
CONV_WIDTH = 3
REL_BUCKETS = 32
NORM_EPS = 1e-6
NEG = -1e30
ROW_TILE = 512
FFN_CHUNK = 256
HALO = 16
VMEM_LIMIT = 56 * 1024 * 1024
HEADS = range(HEADS_PER_STEP)
HEAD_ROWS = [slice(hd * HEAD_DIM, (hd + 1) * HEAD_DIM) for hd in HEADS]

REL_BUCKET_START = (0, 1, 2, 3, 4, 5, 6, 7, 8, 9, 10, 11, 12, 13, 14, 15, 16,
                    19, 21, 24, 27, 31, 35, 40, 46, 52, 59, 67, 77, 87, 99, 113)


def _dot(a, b):
    return jnp.dot(a, b, preferred_element_type=F32)


def _dot_nt(a, b):
    return lax.dot_general(a, b, (((1,), (1,)), ((), ())), preferred_element_type=F32)


def _rms_norm(x, g):
    ms = jnp.mean(x * x, axis=-1, keepdims=True)
    return x * lax.rsqrt(ms + NORM_EPS) * g


def _split_bf16(x):
    hi = x.astype(BF16)
    lo = (x - hi.astype(F32)).astype(BF16)
    return hi, lo


def _params(n_axes):
    return pltpu.CompilerParams(
        dimension_semantics=("arbitrary",) * n_axes,
        vmem_limit_bytes=VMEM_LIMIT)


def _norm_qkv_kernel(x_ref, g_ref, w_ref, q_ref, k_ref, v_ref, *, d_model, q_scale):
    y = _rms_norm(x_ref[...], g_ref[...]).astype(BF16)
    for idx, out in enumerate((q_ref, k_ref, v_ref)):
        r = _dot(y, w_ref[:, idx * d_model:(idx + 1) * d_model])
        if idx == 0:
            r = r * q_scale
        out[...] = r.astype(BF16)


def _norm_qkv(h, gain, w_bf16):
    m, d = h.shape
    row = pl.BlockSpec((ROW_TILE, d), lambda i: (i, 0))
    out = jax.ShapeDtypeStruct((m, d), BF16)
    return pl.pallas_call(
        functools.partial(_norm_qkv_kernel, d_model=d, q_scale=HEAD_DIM ** -0.5),
        grid=(m // ROW_TILE,),
        in_specs=[row,
                  pl.BlockSpec((1, d), lambda i: (0, 0)),
                  pl.BlockSpec((d, 3 * d), lambda i: (0, 0))],
        out_specs=[row, row, row],
        out_shape=[out, out, out],
        compiler_params=_params(1),
        name="norm_qkv",
    )(h, gain.reshape(1, d), w_bf16)


def _bias_table_kernel(rb_ref, out_ref):
    h = pl.program_id(0)
    key = lax.broadcasted_iota(jnp.int32, (ATT_TILE, ATT_TILE), 0)
    qry = lax.broadcasted_iota(jnp.int32, (ATT_TILE, ATT_TILE), 1)
    for delta in range(3):
        dist = jnp.maximum(qry - key + ATT_TILE * delta, 0)
        val = jnp.full((ATT_TILE, ATT_TILE), rb_ref[h, 0], F32)
        for b in range(1, REL_BUCKETS):
            val = jnp.where(dist >= REL_BUCKET_START[b], rb_ref[h, b], val)
        if delta == 0:
            val = val + jnp.where(key <= qry, 0.0, NEG)
        out_ref[0, delta] = val


def _bias_table(rel_bias):
    return pl.pallas_call(
        _bias_table_kernel,
        grid=(N_HEADS,),
        in_specs=[pl.BlockSpec(memory_space=pltpu.SMEM)],
        out_specs=pl.BlockSpec((1, 3, ATT_TILE, ATT_TILE), lambda h: (h, 0, 0, 0)),
        out_shape=jax.ShapeDtypeStruct((N_HEADS, 3, ATT_TILE, ATT_TILE), F32),
        compiler_params=_params(1),
        name="rel_bias_table",
    )(rel_bias)


ITEM_QUERY, ITEM_KEY, ITEM_FIRST, ITEM_LAST = range(4)
ITEM_FIELDS = 4
IDLE_ITEM = (0, 0, 1, 0)
RING = 8
SEQS_PER_STEP = 4
SUBLANES = 8


def _causal_items(n_blocks, nearest, farthest):
    items = []
    for tile in range(SEQS_PER_STEP * n_blocks):
        ts = range(nearest, min(tile % n_blocks, farthest) + 1)
        items += [(tile, tile - t, int(t == ts[0]), int(t == ts[-1])) for t in ts]
    return items


def _item_table(items, n_stages):
    steps = len(items) + n_stages - 1
    steps += -steps % RING
    lead = [IDLE_ITEM] * (n_stages - 1)
    trail = [IDLE_ITEM] * (steps + n_stages - 1 - len(lead) - len(items))
    return np.array(lead + list(items) + trail, np.int32).T, steps // RING


class _Item:
    def __init__(self, tab_ref, i, stage, n_stages):
        idx = i + (n_stages - 1) - stage
        self.query = tab_ref[ITEM_QUERY, idx]
        self.key = tab_ref[ITEM_KEY, idx]
        self.first = tab_ref[ITEM_FIRST, idx]
        self.last = tab_ref[ITEM_LAST, idx]


def _tile_rows(j):
    return pl.ds(pl.multiple_of(j * ATT_TILE, ATT_TILE), ATT_TILE)


def _grouped(x):
    return x.reshape(x.shape[0] // SUBLANES, SUBLANES, x.shape[1])


def _all_sublanes(x, op):
    for shift in (4, 2, 1):
        x = op(x, pltpu.roll(x, shift, 0))
    return x


def _store_head_queries(q_ref, qh_ref):
    q2 = q_ref[...]
    lane = lax.broadcasted_iota(jnp.int32, q2.shape, 1)
    for hd in HEADS:
        keep = (lane < HEAD_DIM) if hd == 0 else (lane >= HEAD_DIM)
        qh_ref[hd] = jnp.where(keep, q2, jnp.zeros_like(q2))


def _store_v_transposed(v_ref, vt_ref):
    for j in range(vt_ref.shape[0]):
        vj = v_ref[j * ATT_TILE:(j + 1) * ATT_TILE, :].astype(F32)
        vt_ref[j] = vj.T.astype(BF16)


def _scores(item, k_ref, qh_ref, hd):
    return _dot_nt(k_ref[_tile_rows(item.key), :], qh_ref[hd, _tile_rows(item.query), :])


def _first_grid_step():
    return (pl.program_id(0) == 0) & (pl.program_id(1) == 0)


def _run_pipeline(step, n_bodies):
    def body(m, carry):
        finish = [step(m * RING + slot, slot) for slot in range(RING)]
        for write_out in finish:
            write_out()
        return carry

    lax.fori_loop(0, n_bodies, body, 0)


def _attention_call(kernel, items, n_stages, seq_len, q, k, v, extra_inputs, extra_specs,
                    scratch, name):
    m, d = q.shape
    rows = SEQS_PER_STEP * seq_len
    n_blocks = seq_len // ATT_TILE
    table, n_bodies = _item_table(items, n_stages)
    seq_spec = pl.BlockSpec((rows, LANES), lambda hp, g: (g, hp))
    return pl.pallas_call(
        functools.partial(kernel, n_blocks=n_blocks, n_bodies=n_bodies),
        grid=(d // LANES, m // rows),
        in_specs=[pl.BlockSpec(memory_space=pltpu.SMEM), seq_spec, seq_spec, seq_spec] + extra_specs,
        out_specs=seq_spec,
        out_shape=jax.ShapeDtypeStruct((m, d), BF16),
        scratch_shapes=[pltpu.VMEM((HEADS_PER_STEP, rows, LANES), BF16),
                        pltpu.VMEM((SEQS_PER_STEP * n_blocks, LANES, ATT_TILE), BF16)]
        + scratch,
        compiler_params=_params(2),
        name=name,
    )(jnp.asarray(table), q, k, v, *extra_inputs)


def _tile_ring(dtype):
    return pltpu.VMEM((RING, HEADS_PER_STEP, ATT_TILE, ATT_TILE), dtype)


def _row_ring():
    return pltpu.VMEM((RING, HEADS_PER_STEP, SUBLANES, ATT_TILE), F32)


def _acc_ring():
    return pltpu.VMEM((RING, HEADS_PER_STEP, HEAD_DIM, ATT_TILE), F32)


def _out_ring():
    return pltpu.VMEM((RING, ATT_TILE, LANES), BF16)


def _zero(*refs):
    for ref in refs:
        ref[...] = jnp.zeros(ref.shape, ref.dtype)


MOBA_STAGES = 3


def _moba_block_choice(seq, k_ref, qh_ref, sel_ref, n_blocks):
    seq_len = n_blocks * ATT_TILE
    seq_rows = pl.ds(pl.multiple_of(seq * seq_len, seq_len), seq_len)
    means = []
    for j in range(n_blocks):
        kj = k_ref[_tile_rows(seq * n_blocks + j), :].astype(F32)
        means.append(jnp.sum(kj, axis=0, keepdims=True) * (1.0 / ATT_TILE))
    means.append(jnp.zeros((16 - n_blocks, LANES), F32))
    km_hi, km_lo = _split_bf16(jnp.concatenate(means, axis=0))

    blk = lax.broadcasted_iota(jnp.int32, (n_blocks, seq_len), 0)
    own = lax.shift_right_logical(lax.broadcasted_iota(jnp.int32, (n_blocks, seq_len), 1),
                                  ATT_TILE.bit_length() - 1)
    own_row = own[0:1, :]
    for hd in HEADS:
        qh = qh_ref[hd, seq_rows, :]
        gate = (_dot_nt(km_hi, qh) + _dot_nt(km_lo, qh))[0:n_blocks]
        gate = jnp.where(blk < own, gate, NEG)
        for j in range(n_blocks):
            gj = gate[j:j + 1, :]
            beats = (gate > gj) | ((gate == gj) & (blk < j))
            rank = jnp.sum(beats.astype(jnp.int32), axis=0, keepdims=True)
            attended = ((rank < MOBA_TOPK) & (own_row > j)) | (own_row == j)
            row = jnp.broadcast_to(jnp.where(attended, 0.0, NEG), (SUBLANES, seq_len))
            for n in range(n_blocks):
                sel_ref[hd, seq * n_blocks + j, n] = row[:, n * ATT_TILE:(n + 1) * ATT_TILE]


def _moba_kernel(tab_ref, q_ref, k_ref, v_ref, bias_ref, o_ref,
                 qh_ref, vt_ref, sel_ref, s_ring, p_ring, alpha_ring, max_ring,
                 denom_ring, acc_ring, out_ring, *, n_blocks, n_bodies):
    @pl.when(_first_grid_step())
    def _():
        _zero(s_ring, p_ring, alpha_ring, max_ring, denom_ring, acc_ring)

    _store_head_queries(q_ref, qh_ref)
    _store_v_transposed(v_ref, vt_ref)

    for seq in range(SEQS_PER_STEP):
        _moba_block_choice(seq, k_ref, qh_ref, sel_ref, n_blocks)

    def step(i, slot):
        prev = (slot - 1) % RING

        it = _Item(tab_ref, i, 0, MOBA_STAGES)
        for hd in HEADS:
            s_ring[slot, hd] = _scores(it, k_ref, qh_ref, hd)

        it = _Item(tab_ref, i, 2, MOBA_STAGES)
        outs = []
        for hd in HEADS:
            acc = jnp.where(it.first == 1, 0.0, acc_ring[prev, hd])
            acc = (_grouped(acc) * alpha_ring[prev, hd][None]).reshape(acc.shape)
            acc = acc + _dot(vt_ref[it.key, HEAD_ROWS[hd], :], p_ring[prev, hd])
            acc_ring[slot, hd] = acc
            outs.append((_grouped(acc) / denom_ring[prev, hd][None]).reshape(acc.shape))
        out_ring[slot] = jnp.concatenate(outs, axis=0).T.astype(BF16)
        done, done_query = it.last, it.query

        it = _Item(tab_ref, i, 1, MOBA_STAGES)
        for hd in HEADS:
            logit = _grouped(s_ring[prev, hd] + bias_ref[hd, jnp.minimum(it.query - it.key, 2)])
            sel = sel_ref[hd, it.key, it.query % n_blocks]
            m_old = jnp.where(it.first == 1, NEG, max_ring[prev, hd])
            m_tile = _all_sublanes(jnp.max(logit, axis=0), jnp.maximum) + sel
            m_new = jnp.maximum(m_old, m_tile)
            alpha = jnp.exp(m_old - m_new)
            p = jnp.exp(logit - (m_new - sel)[None])
            denom_old = jnp.where(it.first == 1, 0.0, denom_ring[prev, hd])
            max_ring[slot, hd] = m_new
            denom_ring[slot, hd] = alpha * denom_old + _all_sublanes(jnp.sum(p, axis=0), jnp.add)
            alpha_ring[slot, hd] = alpha
            p_ring[slot, hd] = p.reshape(ATT_TILE, ATT_TILE).astype(BF16)

        def write_out():
            @pl.when(done == 1)
            def _():
                o_ref[_tile_rows(done_query), :] = out_ring[slot]

        return write_out

    _run_pipeline(step, n_bodies)


def _moba_attention(q, k, v, bias_table, seq_len):
    n_blocks = seq_len // ATT_TILE
    bias_spec = pl.BlockSpec((HEADS_PER_STEP, 3, ATT_TILE, ATT_TILE),
                             lambda hp, g: (hp, 0, 0, 0))
    scratch = [pltpu.VMEM((HEADS_PER_STEP, SEQS_PER_STEP * n_blocks, n_blocks, SUBLANES, ATT_TILE),
                          F32),
               _tile_ring(F32),
               _tile_ring(BF16),
               _row_ring(),
               _row_ring(),
               _row_ring(),
               _acc_ring(),
               _out_ring()]
    return _attention_call(_moba_kernel, _causal_items(n_blocks, 0, n_blocks), MOBA_STAGES,
                           seq_len, q, k, v, [bias_table], [bias_spec], scratch, "moba_attention")


SB_STAGES = 4
SB_NEAR_TILES = 2
SB_SKIP_BELOW = -110.0


def _neg_softplus(z):
    neg_z = -z
    return jnp.minimum(neg_z, 0.0) - jnp.log(1.0 + jnp.exp(jnp.minimum(z, neg_z)))


def _sb_far_items(state_tail_ref, far_tab_ref, n_blocks):
    def put(idx, item):
        for field in range(ITEM_FIELDS):
            far_tab_ref[field, idx] = jnp.int32(item[field])

    lead = SB_STAGES - 1
    for r in range(lead):
        put(r, IDLE_ITEM)
    count = jnp.int32(lead)
    for tile in range(SEQS_PER_STEP * n_blocks):
        n = tile % n_blocks
        if n < SB_NEAR_TILES:
            continue
        items = [(tile, tile - t, int(t == SB_NEAR_TILES), int(t == n))
                 for t in range(SB_NEAR_TILES, n + 1)]
        live = jnp.maximum(jnp.max(state_tail_ref[tile, 0]),
                           jnp.max(state_tail_ref[tile, 1])) >= SB_SKIP_BELOW

        @pl.when(live)
        def _(items=items, count=count):
            for r, item in enumerate(items):
                put(count + r, item)

        count = count + jnp.where(live, len(items), 0)
    for r in range(lead + RING):
        put(count + r, IDLE_ITEM)
    n_items = count - lead
    return jnp.where(n_items > 0, (n_items + lead + RING - 1) // RING, 0)


def _sb_kernel(tab_ref, q_ref, k_ref, v_ref, o_ref,
               qh_ref, vt_ref, tri_ref, mask_ref, z_ring, zm_ring, split_ring, e_ring,
               sum_ring, tail_ring, acc_ring, out_ring, state_tail_ref, state_acc_ref, far_tab_ref,
               *, n_blocks, n_bodies):
    @pl.when(_first_grid_step())
    def _():
        key = lax.broadcasted_iota(jnp.int32, (ATT_TILE, ATT_TILE), 0)
        qry = lax.broadcasted_iota(jnp.int32, (ATT_TILE, ATT_TILE), 1)
        tri = (qry >= key).astype(BF16)
        tri_ref[...] = jnp.concatenate([tri, tri], axis=1)
        mask_ref[0] = jnp.zeros((ATT_TILE, ATT_TILE), F32)
        mask_ref[1] = jnp.where(key < qry, 0.0, NEG)
        _zero(z_ring, zm_ring, split_ring, e_ring, sum_ring, tail_ring, acc_ring)

    _store_head_queries(q_ref, qh_ref)
    _store_v_transposed(v_ref, vt_ref)
    _zero(state_tail_ref, state_acc_ref)

    def pipeline_step(tab):
        def step(i, slot):
            prev = (slot - 1) % RING

            it = _Item(tab, i, 1, SB_STAGES)
            diagonal = (it.query == it.key).astype(jnp.int32)
            for hd in HEADS:
                z = z_ring[prev, hd] + mask_ref[diagonal]
                zm_ring[slot, hd] = z
                hi, lo = _split_bf16(_neg_softplus(z))
                split_ring[slot, hd, 0:ATT_TILE, :] = hi
                split_ring[slot, hd, ATT_TILE:, :] = lo

            it = _Item(tab, i, 0, SB_STAGES)
            for hd in HEADS:
                z_ring[slot, hd] = _scores(it, k_ref, qh_ref, hd)

            for hd in HEADS:
                csum = _dot(tri_ref[...], split_ring[prev, hd])
                e_ring[slot, hd] = zm_ring[prev, hd] + csum
                sum_ring[slot, hd] = jnp.broadcast_to(csum[0:1, :], (SUBLANES, ATT_TILE))

            it = _Item(tab, i, 3, SB_STAGES)
            accs = []
            for hd in HEADS:
                tail = jnp.where(it.first == 1, state_tail_ref[it.query, hd], tail_ring[prev, hd])
                a = jnp.exp(_grouped(e_ring[prev, hd]) + tail[None]).reshape(ATT_TILE, ATT_TILE)
                acc = jnp.where(it.first == 1, state_acc_ref[it.query, hd], acc_ring[prev, hd])
                acc = acc + _dot(vt_ref[it.key, HEAD_ROWS[hd], :], a.astype(BF16))
                acc_ring[slot, hd] = acc
                accs.append(acc)
                tail_ring[slot, hd] = tail + sum_ring[prev, hd]
            out_ring[slot] = jnp.concatenate(accs, axis=0).T.astype(BF16)
            done, done_query = it.last, it.query

            def write_out():
                @pl.when(done == 1)
                def _():
                    for hd in HEADS:
                        state_tail_ref[done_query, hd] = tail_ring[slot, hd]
                        state_acc_ref[done_query, hd] = acc_ring[slot, hd]
                    o_ref[_tile_rows(done_query), :] = out_ring[slot]

            return write_out

        return step

    _run_pipeline(pipeline_step(tab_ref), n_bodies)
    _run_pipeline(pipeline_step(far_tab_ref), _sb_far_items(state_tail_ref, far_tab_ref, n_blocks))


def _sb_attention(q, k, v, seq_len):
    n_blocks = seq_len // ATT_TILE
    n_tiles = SEQS_PER_STEP * n_blocks
    far_entries = (len(_causal_items(n_blocks, SB_NEAR_TILES, n_blocks))
                   + 2 * (SB_STAGES - 1) + RING)
    scratch = [pltpu.VMEM((ATT_TILE, 2 * ATT_TILE), BF16),
               pltpu.VMEM((2, ATT_TILE, ATT_TILE), F32),
               _tile_ring(F32),
               _tile_ring(F32),
               pltpu.VMEM((RING, HEADS_PER_STEP, 2 * ATT_TILE, ATT_TILE), BF16),
               _tile_ring(F32),
               _row_ring(),
               _row_ring(),
               _acc_ring(),
               _out_ring(),
               pltpu.VMEM((n_tiles, HEADS_PER_STEP, SUBLANES, ATT_TILE), F32),
               pltpu.VMEM((n_tiles, HEADS_PER_STEP, HEAD_DIM, ATT_TILE), F32),
               pltpu.SMEM((ITEM_FIELDS, far_entries), jnp.int32)]
    return _attention_call(_sb_kernel, _causal_items(n_blocks, 0, SB_NEAR_TILES - 1), SB_STAGES,
                           seq_len, q, k, v, [], [], scratch, "stick_breaking_attention")


def _wo_conv_ffn_kernel(o_ref, o_halo_ref, h_ref, h_halo_ref, wo_ref, g_ref, wup_ref, cw_ref,
                        wdn_ref, gn_ref, out_ref, oext_ref, yext_ref, acc_ref,
                        *, tiles_per_seq, d_ff, final_norm):
    i = pl.program_id(0)
    seq_start = (i % tiles_per_seq) == 0
    tm = o_ref.shape[0]

    oext_ref[0:HALO, :] = o_halo_ref[...]
    oext_ref[HALO:, :] = o_ref[...]
    proj = _dot(oext_ref[...], wo_ref[...])
    h_halo = h_halo_ref[...] + proj[0:HALO]
    h_tile = h_ref[...] + proj[HALO:]
    y_halo = _rms_norm(h_halo, g_ref[...]).astype(BF16)
    yext_ref[0:HALO, :] = jnp.where(seq_start, jnp.zeros_like(y_halo), y_halo)
    yext_ref[HALO:, :] = _rms_norm(h_tile, g_ref[...]).astype(BF16)
    acc_ref[...] = h_tile

    def conv(u, taps):
        out = taps[CONV_WIDTH:CONV_WIDTH + 1, :]
        for kk in range(CONV_WIDTH):
            lo = HALO - (CONV_WIDTH - 1) + kk
            out = out + taps[kk:kk + 1, :] * u[lo:lo + tm, :]
        return out

    n_chunks = d_ff // FFN_CHUNK
    gate_cols = [slice(c * FFN_CHUNK, (c + 1) * FFN_CHUNK) for c in range(n_chunks)]
    val_cols = [slice(d_ff + c * FFN_CHUNK, d_ff + (c + 1) * FFN_CHUNK) for c in range(n_chunks)]
    up, act = {}, {}
    for step in range(n_chunks + 2):
        c = step
        if c < n_chunks:
            yext = yext_ref[...]
            up[c] = (_dot(yext, wup_ref[:, gate_cols[c]]), _dot(yext, wup_ref[:, val_cols[c]]))
        c = step - 2
        if 0 <= c < n_chunks:
            acc_ref[...] += _dot(act.pop(c), wdn_ref[gate_cols[c], :])
        c = step - 1
        if 0 <= c < n_chunks:
            u_gate, u_val = up.pop(c)
            gate = conv(u_gate, cw_ref[:, gate_cols[c]])
            val = conv(u_val, cw_ref[:, val_cols[c]])
            act[c] = (gate * (1.0 / (1.0 + jnp.exp(-gate))) * val).astype(BF16)
    h = acc_ref[...]
    out_ref[...] = _rms_norm(h, gn_ref[...]) if final_norm else h


def _wo_conv_ffn(o, h, wo, ffn_gain, wup, conv_taps, wdn, final_gain, final_norm, seq_len):
    m, d = h.shape
    d_ff = wdn.shape[0]
    row = pl.BlockSpec((ROW_TILE, d), lambda i: (i, 0))
    halo = pl.BlockSpec((HALO, d), lambda i: (jnp.maximum(i * (ROW_TILE // HALO) - 1, 0), 0))
    gain = pl.BlockSpec((1, d), lambda i: (0, 0))
    whole = lambda a: pl.BlockSpec(a.shape, lambda i: (0,) * a.ndim, pipeline_mode=pl.Buffered(1))
    return pl.pallas_call(
        functools.partial(_wo_conv_ffn_kernel, tiles_per_seq=seq_len // ROW_TILE,
                          d_ff=d_ff, final_norm=final_norm),
        grid=(m // ROW_TILE,),
        in_specs=[row, halo, row, halo, whole(wo), gain, whole(wup), whole(conv_taps), whole(wdn),
                  gain],
        out_specs=row,
        out_shape=jax.ShapeDtypeStruct((m, d), F32),
        scratch_shapes=[pltpu.VMEM((ROW_TILE + HALO, d), BF16),
                        pltpu.VMEM((ROW_TILE + HALO, d), BF16),
                        pltpu.VMEM((ROW_TILE, d), F32)],
        compiler_params=_params(1),
        name="wo_conv_ffn",
    )(o, o, h, h, wo, ffn_gain.reshape(1, d), wup, conv_taps, wdn, final_gain.reshape(1, d))


def kernel(x, attn_norm, w_qkv, w_o, rel_bias, ffn_norm, w_up, conv_w, conv_b, w_down, final_norm):
    b, s, d = x.shape
    depth = w_qkv.shape[0]
    h = x.reshape(b * s, d)
    bias_table = _bias_table(rel_bias)
    for i in range(depth):
        q, k, v = _norm_qkv(h, attn_norm[i], w_qkv[i].astype(BF16))
        if i % 2 == 0:
            o = _moba_attention(q, k, v, bias_table, s)
        else:
            o = _sb_attention(q, k, v, s)
        taps = jnp.concatenate(
            [conv_w[i], conv_b[i][None, :],
             jnp.zeros((8 - CONV_WIDTH - 1, conv_w.shape[-1]), F32)], axis=0)
        h = _wo_conv_ffn(o, h, w_o[i].astype(BF16), ffn_norm[i],
                         w_up[i].astype(BF16), taps, w_down[i].astype(BF16),
                         final_norm, i == depth - 1, s)
    return h.reshape(b, s, d)
```

```python
import functools

import numpy as np

import jax
import jax.numpy as jnp
from jax import lax
from jax.experimental import pallas as pl
from jax.experimental.pallas import tpu as pltpu

F32 = jnp.float32
BF16 = jnp.bfloat16

N_HEADS = 16
HEAD_DIM = 64
HEADS_PER_STEP = 2
LANES = HEADS_PER_STEP * HEAD_DIM
MOBA_BLOCK = 256
MOBA_TOPK = 3
ATT_TILE = MOBA_BLOCK
CONV_WIDTH = 3
REL_BUCKETS = 32
NORM_EPS = 1e-6
NEG = -1e30
ROW_TILE = 512
FFN_CHUNK = 256
HALO = 16
VMEM_LIMIT = 56 * 1024 * 1024
HEADS = range(HEADS_PER_STEP)
HEAD_ROWS = [slice(hd * HEAD_DIM, (hd + 1) * HEAD_DIM) for hd in HEADS]

REL_BUCKET_START = (0, 1, 2, 3, 4, 5, 6, 7, 8, 9, 10, 11, 12, 13, 14, 15, 16,
                    19, 21, 24, 27, 31, 35, 40, 46, 52, 59, 67, 77, 87, 99, 113)


def _dot(a, b):
    return jnp.dot(a, b, preferred_element_type=F32)


def _dot_nt(a, b):
    return lax.dot_general(a, b, (((1,), (1,)), ((), ())), preferred_element_type=F32)


def _rms_norm(x, g):
    ms = jnp.mean(x * x, axis=-1, keepdims=True)
    return x * lax.rsqrt(ms + NORM_EPS) * g


def _split_bf16(x):
    hi = x.astype(BF16)
    lo = (x - hi.astype(F32)).astype(BF16)
    return hi, lo


def _params(n_axes):
    return pltpu.CompilerParams(
        dimension_semantics=("arbitrary",) * n_axes,
        vmem_limit_bytes=VMEM_LIMIT)


def _norm_qkv_kernel(x_ref, g_ref, w_ref, q_ref, k_ref, v_ref, *, d_model, q_scale):
    y = _rms_norm(x_ref[...], g_ref[...]).astype(BF16)
    for idx, out in enumerate((q_ref, k_ref, v_ref)):
        r = _dot(y, w_ref[:, idx * d_model:(idx + 1) * d_model])
        if idx == 0:
            r = r * q_scale
        out[...] = r.astype(BF16)


def _norm_qkv(h, gain, w_bf16):
    m, d = h.shape
    row = pl.BlockSpec((ROW_TILE, d), lambda i: (i, 0))
    out = jax.ShapeDtypeStruct((m, d), BF16)
    return pl.pallas_call(
        functools.partial(_norm_qkv_kernel, d_model=d, q_scale=HEAD_DIM ** -0.5),
        grid=(m // ROW_TILE,),
        in_specs=[row,
                  pl.BlockSpec((1, d), lambda i: (0, 0)),
                  pl.BlockSpec((d, 3 * d), lambda i: (0, 0))],
        out_specs=[row, row, row],
        out_shape=[out, out, out],
        compiler_params=_params(1),
        name="norm_qkv",
    )(h, gain.reshape(1, d), w_bf16)


def _bias_table_kernel(rb_ref, out_ref):
    h = pl.program_id(0)
    key = lax.broadcasted_iota(jnp.int32, (ATT_TILE, ATT_TILE), 0)
    qry = lax.broadcasted_iota(jnp.int32, (ATT_TILE, ATT_TILE), 1)
    for delta in range(3):
        dist = jnp.maximum(qry - key + ATT_TILE * delta, 0)
        val = jnp.full((ATT_TILE, ATT_TILE), rb_ref[h, 0], F32)
        for b in range(1, REL_BUCKETS):
            val = jnp.where(dist >= REL_BUCKET_START[b], rb_ref[h, b], val)
        if delta == 0:
            val = val + jnp.where(key <= qry, 0.0, NEG)
        out_ref[0, delta] = val


def _bias_table(rel_bias):
    return pl.pallas_call(
        _bias_table_kernel,
        grid=(N_HEADS,),
        in_specs=[pl.BlockSpec(memory_space=pltpu.SMEM)],
        out_specs=pl.BlockSpec((1, 3, ATT_TILE, ATT_TILE), lambda h: (h, 0, 0, 0)),
        out_shape=jax.ShapeDtypeStruct((N_HEADS, 3, ATT_TILE, ATT_TILE), F32),
        compiler_params=_params(1),
        name="rel_bias_table",
    )(rel_bias)


ITEM_QUERY, ITEM_KEY, ITEM_FIRST, ITEM_LAST = range(4)
ITEM_FIELDS = 4
IDLE_ITEM = (0, 0, 1, 0)
RING = 8
SEQS_PER_STEP = 4
SUBLANES = 8


def _causal_items(n_blocks, nearest, farthest):
    items = []
    for tile in range(SEQS_PER_STEP * n_blocks):
        ts = range(nearest, min(tile % n_blocks, farthest) + 1)
        items += [(tile, tile - t, int(t == ts[0]), int(t == ts[-1])) for t in ts]
    return items


def _item_table(items, n_stages):
    steps = len(items) + n_stages - 1
    steps += -steps % RING
    lead = [IDLE_ITEM] * (n_stages - 1)
    trail = [IDLE_ITEM] * (steps + n_stages - 1 - len(lead) - len(items))
    return np.array(lead + list(items) + trail, np.int32).T, steps // RING


class _Item:
    def __init__(self, tab_ref, i, stage, n_stages):
        idx = i + (n_stages - 1) - stage
        self.query = tab_ref[ITEM_QUERY, idx]
        self.key = tab_ref[ITEM_KEY, idx]
        self.first = tab_ref[ITEM_FIRST, idx]
        self.last = tab_ref[ITEM_LAST, idx]


def _tile_rows(j):
    return pl.ds(pl.multiple_of(j * ATT_TILE, ATT_TILE), ATT_TILE)


def _grouped(x):
    return x.reshape(x.shape[0] // SUBLANES, SUBLANES, x.shape[1])


def _all_sublanes(x, op):
    for shift in (4, 2, 1):
        x = op(x, pltpu.roll(x, shift, 0))
    return x


def _store_head_queries(q_ref, qh_ref):
    q2 = q_ref[...]
    lane = lax.broadcasted_iota(jnp.int32, q2.shape, 1)
    for hd in HEADS:
        keep = (lane < HEAD_DIM) if hd == 0 else (lane >= HEAD_DIM)
        qh_ref[hd] = jnp.where(keep, q2, jnp.zeros_like(q2))


def _store_v_transposed(v_ref, vt_ref):
    for j in range(vt_ref.shape[0]):
        vj = v_ref[j * ATT_TILE:(j + 1) * ATT_TILE, :].astype(F32)
        vt_ref[j] = vj.T.astype(BF16)


def _scores(item, k_ref, qh_ref, hd):
    return _dot_nt(k_ref[_tile_rows(item.key), :], qh_ref[hd, _tile_rows(item.query), :])


def _first_grid_step():
    return (pl.program_id(0) == 0) & (pl.program_id(1) == 0)


def _run_pipeline(step, n_bodies):
    def body(m, carry):
        finish = [step(m * RING + slot, slot) for slot in range(RING)]
        for write_out in finish:
            write_out()
        return carry

    lax.fori_loop(0, n_bodies, body, 0)


def _attention_call(kernel, items, n_stages, seq_len, q, k, v, extra_inputs, extra_specs,
                    scratch, name):
    m, d = q.shape
    rows = SEQS_PER_STEP * seq_len
    n_blocks = seq_len // ATT_TILE
    table, n_bodies = _item_table(items, n_stages)
    seq_spec = pl.BlockSpec((rows, LANES), lambda hp, g: (g, hp))
    return pl.pallas_call(
        functools.partial(kernel, n_blocks=n_blocks, n_bodies=n_bodies),
        grid=(d // LANES, m // rows),
        in_specs=[pl.BlockSpec(memory_space=pltpu.SMEM), seq_spec, seq_spec, seq_spec] + extra_specs,
        out_specs=seq_spec,
        out_shape=jax.ShapeDtypeStruct((m, d), BF16),
        scratch_shapes=[pltpu.VMEM((HEADS_PER_STEP, rows, LANES), BF16),
                        pltpu.VMEM((SEQS_PER_STEP * n_blocks, LANES, ATT_TILE), BF16)]
        + scratch,
        compiler_params=_params(2),
        name=name,
    )(jnp.asarray(table), q, k, v, *extra_inputs)


def _tile_ring(dtype):
    return pltpu.VMEM((RING, HEADS_PER_STEP, ATT_TILE, ATT_TILE), dtype)


def _row_ring():
    return pltpu.VMEM((RING, HEADS_PER_STEP, SUBLANES, ATT_TILE), F32)


def _acc_ring():
    return pltpu.VMEM((RING, HEADS_PER_STEP, HEAD_DIM, ATT_TILE), F32)


def _out_ring():
    return pltpu.VMEM((RING, ATT_TILE, LANES), BF16)


def _zero(*refs):
    for ref in refs:
        ref[...] = jnp.zeros(ref.shape, ref.dtype)


MOBA_STAGES = 3


def _moba_block_choice(seq, k_ref, qh_ref, sel_ref, n_blocks):
    seq_len = n_blocks * ATT_TILE
    seq_rows = pl.ds(pl.multiple_of(seq * seq_len, seq_len), seq_len)
    means = []
    for j in range(n_blocks):
        kj = k_ref[_tile_rows(seq * n_blocks + j), :].astype(F32)
        means.append(jnp.sum(kj, axis=0, keepdims=True) * (1.0 / ATT_TILE))
    means.append(jnp.zeros((16 - n_blocks, LANES), F32))
    km_hi, km_lo = _split_bf16(jnp.concatenate(means, axis=0))

    blk = lax.broadcasted_iota(jnp.int32, (n_blocks, seq_len), 0)
    own = lax.shift_right_logical(lax.broadcasted_iota(jnp.int32, (n_blocks, seq_len), 1),
                                  ATT_TILE.bit_length() - 1)
    own_row = own[0:1, :]
    for hd in HEADS:
        qh = qh_ref[hd, seq_rows, :]
        gate = (_dot_nt(km_hi, qh) + _dot_nt(km_lo, qh))[0:n_blocks]
        gate = jnp.where(blk < own, gate, NEG)
        for j in range(n_blocks):
            gj = gate[j:j + 1, :]
            beats = (gate > gj) | ((gate == gj) & (blk < j))
            rank = jnp.sum(beats.astype(jnp.int32), axis=0, keepdims=True)
            attended = ((rank < MOBA_TOPK) & (own_row > j)) | (own_row == j)
            row = jnp.broadcast_to(jnp.where(attended, 0.0, NEG), (SUBLANES, seq_len))
            for n in range(n_blocks):
                sel_ref[hd, seq * n_blocks + j, n] = row[:, n * ATT_TILE:(n + 1) * ATT_TILE]


def _moba_kernel(tab_ref, q_ref, k_ref, v_ref, bias_ref, o_ref,
                 qh_ref, vt_ref, sel_ref, s_ring, p_ring, alpha_ring, max_ring,
                 denom_ring, done_denom_ring, acc_ring, *, n_blocks, n_bodies):
    @pl.when(_first_grid_step())
    def _():
        _zero(s_ring, p_ring, alpha_ring, max_ring, denom_ring, done_denom_ring, acc_ring)

    _store_head_queries(q_ref, qh_ref)
    _store_v_transposed(v_ref, vt_ref)

    for seq in range(SEQS_PER_STEP):
        _moba_block_choice(seq, k_ref, qh_ref, sel_ref, n_blocks)

    def step(i, slot):
        prev = (slot - 1) % RING

        it = _Item(tab_ref, i, 0, MOBA_STAGES)
        for hd in HEADS:
            s_ring[slot, hd] = _scores(it, k_ref, qh_ref, hd)

        it = _Item(tab_ref, i, 2, MOBA_STAGES)
        for hd in HEADS:
            acc = jnp.where(it.first == 1, 0.0, acc_ring[prev, hd])
            acc = (_grouped(acc) * alpha_ring[prev, hd][None]).reshape(acc.shape)
            acc_ring[slot, hd] = acc + _dot(vt_ref[it.key, HEAD_ROWS[hd], :], p_ring[prev, hd])
            done_denom_ring[slot, hd] = denom_ring[prev, hd]
        done, done_query = it.last, it.query

        it = _Item(tab_ref, i, 1, MOBA_STAGES)
        for hd in HEADS:
            logit = _grouped(s_ring[prev, hd] + bias_ref[hd, jnp.minimum(it.query - it.key, 2)])
            sel = sel_ref[hd, it.key, it.query % n_blocks]
            m_old = jnp.where(it.first == 1, NEG, max_ring[prev, hd])
            m_tile = _all_sublanes(jnp.max(logit, axis=0), jnp.maximum) + sel
            m_new = jnp.maximum(m_old, m_tile)
            alpha = jnp.exp(m_old - m_new)
            p = jnp.exp(logit - (m_new - sel)[None])
            denom_old = jnp.where(it.first == 1, 0.0, denom_ring[prev, hd])
            max_ring[slot, hd] = m_new
            denom_ring[slot, hd] = alpha * denom_old + _all_sublanes(jnp.sum(p, axis=0), jnp.add)
            alpha_ring[slot, hd] = alpha
            p_ring[slot, hd] = p.reshape(ATT_TILE, ATT_TILE).astype(BF16)

        def write_out():
            @pl.when(done == 1)
            def _():
                out = [(_grouped(acc_ring[slot, hd]) / done_denom_ring[slot, hd][None])
                       .reshape(HEAD_DIM, ATT_TILE) for hd in HEADS]
                o_ref[_tile_rows(done_query), :] = jnp.concatenate(out, axis=0).T.astype(BF16)

        return write_out

    _run_pipeline(step, n_bodies)


def _moba_attention(q, k, v, bias_table, seq_len):
    n_blocks = seq_len // ATT_TILE
    bias_spec = pl.BlockSpec((HEADS_PER_STEP, 3, ATT_TILE, ATT_TILE),
                             lambda hp, g: (hp, 0, 0, 0))
    scratch = [pltpu.VMEM((HEADS_PER_STEP, SEQS_PER_STEP * n_blocks, n_blocks, SUBLANES, ATT_TILE),
                          F32),
               _tile_ring(F32),
               _tile_ring(BF16),
               _row_ring(),
               _row_ring(),
               _row_ring(),
               _row_ring(),
               _acc_ring()]
    return _attention_call(_moba_kernel, _causal_items(n_blocks, 0, n_blocks), MOBA_STAGES,
                           seq_len, q, k, v, [bias_table], [bias_spec], scratch, "moba_attention")


SB_STAGES = 4
SB_NEAR_TILES = 2
SB_SKIP_BELOW = -110.0


def _neg_softplus(z):
    neg_z = -z
    return jnp.minimum(neg_z, 0.0) - jnp.log(1.0 + jnp.exp(jnp.minimum(z, neg_z)))


def _sb_far_items(state_tail_ref, far_tab_ref, n_blocks):
    def put(idx, item):
        for field in range(ITEM_FIELDS):
            far_tab_ref[field, idx] = jnp.int32(item[field])

    lead = SB_STAGES - 1
    for r in range(lead):
        put(r, IDLE_ITEM)
    count = jnp.int32(lead)
    for tile in range(SEQS_PER_STEP * n_blocks):
        n = tile % n_blocks
        if n < SB_NEAR_TILES:
            continue
        items = [(tile, tile - t, int(t == SB_NEAR_TILES), int(t == n))
                 for t in range(SB_NEAR_TILES, n + 1)]
        live = jnp.maximum(jnp.max(state_tail_ref[tile, 0]),
                           jnp.max(state_tail_ref[tile, 1])) >= SB_SKIP_BELOW

        @pl.when(live)
        def _(items=items, count=count):
            for r, item in enumerate(items):
                put(count + r, item)

        count = count + jnp.where(live, len(items), 0)
    for r in range(lead + RING):
        put(count + r, IDLE_ITEM)
    n_items = count - lead
    return jnp.where(n_items > 0, (n_items + lead + RING - 1) // RING, 0)


def _sb_kernel(tab_ref, q_ref, k_ref, v_ref, o_ref,
               qh_ref, vt_ref, tri_ref, mask_ref, z_ring, zm_ring, split_ring, e_ring,
               sum_ring, tail_ring, acc_ring, out_ring, state_tail_ref, state_acc_ref, far_tab_ref,
               *, n_blocks, n_bodies):
    @pl.when(_first_grid_step())
    def _():
        key = lax.broadcasted_iota(jnp.int32, (ATT_TILE, ATT_TILE), 0)
        qry = lax.broadcasted_iota(jnp.int32, (ATT_TILE, ATT_TILE), 1)
        tri = (qry >= key).astype(BF16)
        tri_ref[...] = jnp.concatenate([tri, tri], axis=1)
        mask_ref[0] = jnp.zeros((ATT_TILE, ATT_TILE), F32)
        mask_ref[1] = jnp.where(key < qry, 0.0, NEG)
        _zero(z_ring, zm_ring, split_ring, e_ring, sum_ring, tail_ring, acc_ring)

    _store_head_queries(q_ref, qh_ref)
    _store_v_transposed(v_ref, vt_ref)
    _zero(state_tail_ref, state_acc_ref)

    def pipeline_step(tab):
        def step(i, slot):
            prev = (slot - 1) % RING

            it = _Item(tab, i, 1, SB_STAGES)
            diagonal = (it.query == it.key).astype(jnp.int32)
            for hd in HEADS:
                z = z_ring[prev, hd] + mask_ref[diagonal]
                zm_ring[slot, hd] = z
                hi, lo = _split_bf16(_neg_softplus(z))
                split_ring[slot, hd, 0:ATT_TILE, :] = hi
                split_ring[slot, hd, ATT_TILE:, :] = lo

            it = _Item(tab, i, 0, SB_STAGES)
            for hd in HEADS:
                z_ring[slot, hd] = _scores(it, k_ref, qh_ref, hd)

            for hd in HEADS:
                csum = _dot(tri_ref[...], split_ring[prev, hd])
                e_ring[slot, hd] = zm_ring[prev, hd] + csum
                sum_ring[slot, hd] = jnp.broadcast_to(csum[0:1, :], (SUBLANES, ATT_TILE))

            it = _Item(tab, i, 3, SB_STAGES)
            accs = []
            for hd in HEADS:
                tail = jnp.where(it.first == 1, state_tail_ref[it.query, hd], tail_ring[prev, hd])
                a = jnp.exp(_grouped(e_ring[prev, hd]) + tail[None]).reshape(ATT_TILE, ATT_TILE)
                acc = jnp.where(it.first == 1, state_acc_ref[it.query, hd], acc_ring[prev, hd])
                acc = acc + _dot(vt_ref[it.key, HEAD_ROWS[hd], :], a.astype(BF16))
                acc_ring[slot, hd] = acc
                accs.append(acc)
                tail_ring[slot, hd] = tail + sum_ring[prev, hd]
            out_ring[slot] = jnp.concatenate(accs, axis=0).T.astype(BF16)
            done, done_query = it.last, it.query

            def write_out():
                @pl.when(done == 1)
                def _():
                    for hd in HEADS:
                        state_tail_ref[done_query, hd] = tail_ring[slot, hd]
                        state_acc_ref[done_query, hd] = acc_ring[slot, hd]
                    o_ref[_tile_rows(done_query), :] = out_ring[slot]

            return write_out

        return step

    _run_pipeline(pipeline_step(tab_ref), n_bodies)
    _run_pipeline(pipeline_step(far_tab_ref), _sb_far_items(state_tail_ref, far_tab_ref, n_blocks))


def _sb_attention(q, k, v, seq_len):
    n_blocks = seq_len // ATT_TILE
    n_tiles = SEQS_PER_STEP * n_blocks
    far_entries = (len(_causal_items(n_blocks, SB_NEAR_TILES, n_blocks))
                   + 2 * (SB_STAGES - 1) + RING)
    scratch = [pltpu.VMEM((ATT_TILE, 2 * ATT_TILE), BF16),
               pltpu.VMEM((2, ATT_TILE, ATT_TILE), F32),
               _tile_ring(F32),
               _tile_ring(F32),
               pltpu.VMEM((RING, HEADS_PER_STEP, 2 * ATT_TILE, ATT_TILE), BF16),
               _tile_ring(F32),
               _row_ring(),
               _row_ring(),
               _acc_ring(),
               _out_ring(),
               pltpu.VMEM((n_tiles, HEADS_PER_STEP, SUBLANES, ATT_TILE), F32),
               pltpu.VMEM((n_tiles, HEADS_PER_STEP, HEAD_DIM, ATT_TILE), F32),
               pltpu.SMEM((ITEM_FIELDS, far_entries), jnp.int32)]
    return _attention_call(_sb_kernel, _causal_items(n_blocks, 0, SB_NEAR_TILES - 1), SB_STAGES,
                           seq_len, q, k, v, [], [], scratch, "stick_breaking_attention")


def _wo_conv_ffn_kernel(o_ref, o_halo_ref, h_ref, h_halo_ref, wo_ref, g_ref, wup_ref, cw_ref,
                        wdn_ref, gn_ref, out_ref, oext_ref, yext_ref, acc_ref,
                        *, tiles_per_seq, d_ff, final_norm):
    i = pl.program_id(0)
    seq_start = (i % tiles_per_seq) == 0
    tm = o_ref.shape[0]

    oext_ref[0:HALO, :] = o_halo_ref[...]
    oext_ref[HALO:, :] = o_ref[...]
    proj = _dot(oext_ref[...], wo_ref[...])
    h_halo = h_halo_ref[...] + proj[0:HALO]
    h_tile = h_ref[...] + proj[HALO:]
    y_halo = _rms_norm(h_halo, g_ref[...]).astype(BF16)
    yext_ref[0:HALO, :] = jnp.where(seq_start, jnp.zeros_like(y_halo), y_halo)
    yext_ref[HALO:, :] = _rms_norm(h_tile, g_ref[...]).astype(BF16)
    acc_ref[...] = h_tile

    def conv(u, taps):
        out = taps[CONV_WIDTH:CONV_WIDTH + 1, :]
        for kk in range(CONV_WIDTH):
            lo = HALO - (CONV_WIDTH - 1) + kk
            out = out + taps[kk:kk + 1, :] * u[lo:lo + tm, :]
        return out

    n_chunks = d_ff // FFN_CHUNK
    gate_cols = [slice(c * FFN_CHUNK, (c + 1) * FFN_CHUNK) for c in range(n_chunks)]
    val_cols = [slice(d_ff + c * FFN_CHUNK, d_ff + (c + 1) * FFN_CHUNK) for c in range(n_chunks)]
    up, act = {}, {}
    for step in range(n_chunks + 2):
        c = step
        if c < n_chunks:
            yext = yext_ref[...]
            up[c] = (_dot(yext, wup_ref[:, gate_cols[c]]), _dot(yext, wup_ref[:, val_cols[c]]))
        c = step - 2
        if 0 <= c < n_chunks:
            acc_ref[...] += _dot(act.pop(c), wdn_ref[gate_cols[c], :])
        c = step - 1
        if 0 <= c < n_chunks:
            u_gate, u_val = up.pop(c)
            gate = conv(u_gate, cw_ref[:, gate_cols[c]])
            val = conv(u_val, cw_ref[:, val_cols[c]])
            act[c] = (gate * (1.0 / (1.0 + jnp.exp(-gate))) * val).astype(BF16)
    h = acc_ref[...]
    out_ref[...] = _rms_norm(h, gn_ref[...]) if final_norm else h


def _wo_conv_ffn(o, h, wo, ffn_gain, wup, conv_taps, wdn, final_gain, final_norm, seq_len):
    m, d = h.shape
    d_ff = wdn.shape[0]
    row = pl.BlockSpec((ROW_TILE, d), lambda i: (i, 0))
    halo = pl.BlockSpec((HALO, d), lambda i: (jnp.maximum(i * (ROW_TILE // HALO) - 1, 0), 0))
    gain = pl.BlockSpec((1, d), lambda i: (0, 0))
    whole = lambda a: pl.BlockSpec(a.shape, lambda i: (0,) * a.ndim, pipeline_mode=pl.Buffered(1))
    return pl.pallas_call(
        functools.partial(_wo_conv_ffn_kernel, tiles_per_seq=seq_len // ROW_TILE,
                          d_ff=d_ff, final_norm=final_norm),
        grid=(m // ROW_TILE,),
        in_specs=[row, halo, row, halo, whole(wo), gain, whole(wup), whole(conv_taps), whole(wdn),
                  gain],
        out_specs=row,
        out_shape=jax.ShapeDtypeStruct((m, d), F32),
        scratch_shapes=[pltpu.VMEM((ROW_TILE + HALO, d), BF16),
                        pltpu.VMEM((ROW_TILE + HALO, d), BF16),
                        pltpu.VMEM((ROW_TILE, d), F32)],
        compiler_params=_params(1),
        name="wo_conv_ffn",
    )(o, o, h, h, wo, ffn_gain.reshape(1, d), wup, conv_taps, wdn, final_gain.reshape(1, d))


def kernel(x, attn_norm, w_qkv, w_o, rel_bias, ffn_norm, w_up, conv_w, conv_b, w_down, final_norm):
    b, s, d = x.shape
    depth = w_qkv.shape[0]
    h = x.reshape(b * s, d)
    bias_table = _bias_table(rel_bias)
    for i in range(depth):
        q, k, v = _norm_qkv(h, attn_norm[i], w_qkv[i].astype(BF16))
        if i % 2 == 0:
            o = _moba_attention(q, k, v, bias_table, s)
        else:
            o = _sb_attention(q, k, v, s)
        taps = jnp.concatenate(
            [conv_w[i], conv_b[i][None, :],
             jnp.zeros((8 - CONV_WIDTH - 1, conv_w.shape[-1]), F32)], axis=0)
        h = _wo_conv_ffn(o, h, w_o[i].astype(BF16), ffn_norm[i],
                         w_up[i].astype(BF16), taps, w_down[i].astype(BF16),
                         final_norm, i == depth - 1, s)
    return h.reshape(b, s, d)
```
